```python
import math
import jax, jax.numpy as jnp
from jax import lax
import numpy as np

D_MODEL = 1024
BATCH = 16
SEQ = 2048
DEPTH = 2

MIX_WIDTH = D_MODEL
HEAD_DIM = 64
NSA_HEADS = 8
NSA_KV_GROUPS = 2
HEADS_PER_GROUP = NSA_HEADS // NSA_KV_GROUPS
NSA_WIDTH = NSA_HEADS * HEAD_DIM
KV_WIDTH = NSA_KV_GROUPS * HEAD_DIM
CONV_WIDTH = MIX_WIDTH - NSA_WIDTH
CONV_GROUPS = CONV_WIDTH // HEAD_DIM
CONV_K = 3
CMP_STRIDE = 16
CMP_BLOCK = 2 * CMP_STRIDE
CMP_HIDDEN = 128
SLC_BLOCK = 64
SLC_TOPK = 8
WINDOW = 256
Q_BLOCK = 64
FORCE_BONUS = 1.0e4
N_BUCKETS = 32
MAX_DISTANCE = 128
MEM_HEADS = 4
MEM_HEAD_DIM = 64
MEM_WIDTH = MEM_HEADS * MEM_HEAD_DIM
MEM_LEN = 256
FFN_HIDDEN = -(-8 * D_MODEL // (3 * 256)) * 256
RMS_EPS = 1e-6
NEG_INF = -1e30
IN_SIZES = (NSA_WIDTH,) + (KV_WIDTH,) * 6 + (3 * NSA_HEADS,) + (CONV_WIDTH,) * 3
IN_WIDTH = sum(IN_SIZES)
IN_SPLITS = tuple(int(s) for s in np.cumsum(IN_SIZES)[:-1])

kernel_name = "hymba_nsa_shortconv_hybrid"


def rmsnorm(x, g):
    xf = x.astype(jnp.float32)
    y = xf * lax.rsqrt(jnp.mean(xf * xf, axis=-1, keepdims=True) + RMS_EPS)
    return (y * g.astype(jnp.float32)).astype(x.dtype)


def t5_bucket(dist):
    n = jnp.maximum(dist, 0)
    max_exact = N_BUCKETS // 2
    nf = jnp.maximum(n, 1).astype(jnp.float32)
    large = max_exact + (jnp.log(nf / max_exact) / math.log(MAX_DISTANCE / max_exact)
                         * (N_BUCKETS - max_exact)).astype(jnp.int32)
    large = jnp.minimum(large, N_BUCKETS - 1)
    return jnp.where(n < max_exact, n, large)


def masked_softmax(logits, mask, axis):
    maskf = mask.astype(jnp.float32)
    logits = jnp.where(mask, logits, NEG_INF)
    m = jnp.max(logits, axis=axis, keepdims=True)
    e = jnp.exp(logits - m) * maskf
    return e / jnp.maximum(jnp.sum(e, axis=axis, keepdims=True), 1e-30)


def compress_blocks(kv, pe, w1, w2):
    B, S, G, Dh = kv.shape
    ch = kv.reshape(B, S // CMP_STRIDE, CMP_STRIDE, G, Dh)
    blocks = jnp.concatenate([ch[:, :-1], ch[:, 1:]], axis=2)
    blocks = blocks + pe[None, None, :, None, :]
    n = blocks.shape[1]
    flat = blocks.transpose(0, 1, 3, 2, 4).reshape(B, n, G, CMP_BLOCK * Dh)
    return jax.nn.gelu(flat @ w1) @ w2


def nsa_attention(q, k_c, v_c, k_s, v_s, k_w, v_w, gates, rel_bias, pe, ck1, ck2, cv1, cv2):
    B, S, G, HPG, Dh = q.shape
    scale = Dh ** -0.5
    kcmp = compress_blocks(k_c, pe, ck1, ck2)
    vcmp = compress_blocks(v_c, pe, cv1, cv2)
    n_cmp = kcmp.shape[1]
    cmp_end = jnp.arange(n_cmp, dtype=jnp.int32) * CMP_STRIDE + CMP_BLOCK - 1
    n_slc = S // SLC_BLOCK
    topk = min(SLC_TOPK, n_slc)
    cs = np.arange(n_cmp) * CMP_STRIDE
    ss = np.arange(n_slc) * SLC_BLOCK
    ov = np.clip(np.minimum(cs[:, None] + CMP_BLOCK, ss[None, :] + SLC_BLOCK)
                 - np.maximum(cs[:, None], ss[None, :]), 0, None) / CMP_BLOCK
    ov = jnp.asarray(ov, dtype=jnp.float32)
    ks_blk = k_s.reshape(B, n_slc, SLC_BLOCK, G, Dh).transpose(0, 3, 1, 2, 4)
    vs_blk = v_s.reshape(B, n_slc, SLC_BLOCK, G, Dh).transpose(0, 3, 1, 2, 4)
    gather_blocks = jax.vmap(jax.vmap(lambda kb, ix: kb[ix]))
    rb_sel = rel_bias.reshape(G, HPG, N_BUCKETS).transpose(0, 2, 1)
    g_index = jnp.arange(G)[None, None, :, None, None]
    kw_pad = jnp.pad(k_w, ((0, 0), (WINDOW, 0), (0, 0), (0, 0)))
    vw_pad = jnp.pad(v_w, ((0, 0), (WINDOW, 0), (0, 0), (0, 0)))
    kw_len = WINDOW + Q_BLOCK
    dist_w = jnp.arange(Q_BLOCK)[:, None] + WINDOW - jnp.arange(kw_len)[None, :]
    band = (dist_w >= 0) & (dist_w < WINDOW)
    bias_w = rel_bias[:, t5_bucket(dist_w)].reshape(G, HPG, Q_BLOCK, kw_len).transpose(2, 0, 1, 3)

    def chunk(args):
        c, q_c, g_c = args
        t = c * Q_BLOCK + jnp.arange(Q_BLOCK, dtype=jnp.int32)
        dist_c = t[:, None] - cmp_end[None, :]
        bias_c = rel_bias[:, t5_bucket(dist_c)].reshape(G, HPG, Q_BLOCK, n_cmp).transpose(2, 0, 1, 3)
        lg = jnp.einsum('bqghd,bngd->bqghn', q_c, kcmp).astype(jnp.float32) * scale + bias_c
        p_cmp = masked_softmax(lg, (dist_c >= 0)[:, None, None, :], -1)
        o_cmp = jnp.einsum('bqghn,bngd->bqghd', p_cmp, vcmp.astype(jnp.float32))
        imp = jnp.einsum('bqghn,nj->bqgj', p_cmp, ov)
        blk = jnp.arange(n_slc, dtype=jnp.int32)[None, :]
        cur = (t // SLC_BLOCK)[:, None]
        valid = blk <= cur
        forced = (blk == 0) | (blk == cur) | (blk == cur - 1)
        score = jnp.where(valid[:, None, :], imp + jnp.where(forced, FORCE_BONUS, 0.0)[:, None, :], NEG_INF)
        top_val, top_idx = lax.top_k(score, topk)
        sel_ok = top_val > 0.5 * NEG_INF
        idx_bg = top_idx.transpose(0, 2, 1, 3)
        k_sel = gather_blocks(ks_blk, idx_bg)
        v_sel = gather_blocks(vs_blk, idx_bg)
        pos = top_idx[..., None] * SLC_BLOCK + jnp.arange(SLC_BLOCK, dtype=jnp.int32)
        dist_s = t[None, :, None, None, None] - pos
        mask_s = sel_ok[..., None] & (dist_s >= 0)
        bias_s = jnp.moveaxis(rb_sel[g_index, t5_bucket(dist_s)], -1, 3)
        lg = jnp.einsum('bqghd,bgqkjd->bqghkj', q_c, k_sel).astype(jnp.float32) * scale + bias_s
        p_sel = masked_softmax(lg, mask_s[:, :, :, None], (-2, -1))
        o_sel = jnp.einsum('bqghkj,bgqkjd->bqghd', p_sel, v_sel.astype(jnp.float32))
        kw_c = lax.dynamic_slice_in_dim(kw_pad, c * Q_BLOCK, kw_len, axis=1)
        vw_c = lax.dynamic_slice_in_dim(vw_pad, c * Q_BLOCK, kw_len, axis=1)
        s_pos = c * Q_BLOCK - WINDOW + jnp.arange(kw_len, dtype=jnp.int32)
        mask_w = band & (s_pos >= 0)[None, :]
        lg = jnp.einsum('bqghd,bkgd->bqghk', q_c, kw_c).astype(jnp.float32) * scale + bias_w
        p_win = masked_softmax(lg, mask_w[:, None, None, :], -1)
        o_win = jnp.einsum('bqghk,bkgd->bqghd', p_win, vw_c.astype(jnp.float32))
        out = g_c[..., 0:1] * o_cmp + g_c[..., 1:2] * o_sel + g_c[..., 2:3] * o_win
        return out.astype(q.dtype)

    nc = S // Q_BLOCK
    q_chunks = jnp.moveaxis(q.reshape(B, nc, Q_BLOCK, G, HPG, Dh), 1, 0)
    g_chunks = jnp.moveaxis(gates.reshape(B, nc, Q_BLOCK, G, HPG, 3), 1, 0)
    out = lax.map(chunk, (jnp.arange(nc, dtype=jnp.int32), q_chunks, g_chunks))
    return jnp.moveaxis(out, 0, 1).reshape(B, S, G, HPG, Dh)


def hybrid_mixer(h, w_in, w_out, gn_nsa, gn_conv, conv_w, rel_bias, pe, ck1, ck2, cv1, cv2):
    B, S, _ = h.shape
    proj = h @ w_in
    q, kc, vc, ks, vs, kw, vw, gl, cb, cc, cx = jnp.split(proj, IN_SPLITS, axis=-1)
    G, HPG, Dh = NSA_KV_GROUPS, HEADS_PER_GROUP, HEAD_DIM
    kvr = lambda a: a.reshape(B, S, G, Dh)
    gates = jax.nn.sigmoid(gl.astype(jnp.float32)).reshape(B, S, G, HPG, 3)
    o_nsa = nsa_attention(q.reshape(B, S, G, HPG, Dh), kvr(kc), kvr(vc), kvr(ks), kvr(vs),
                          kvr(kw), kvr(vw), gates, rel_bias, pe, ck1, ck2, cv1, cv2)
    o_nsa = o_nsa.reshape(B, S, NSA_WIDTH)
    u = cc * cx
    u_pad = jnp.pad(u, ((0, 0), (CONV_K - 1, 0), (0, 0)))
    y = sum(conv_w[k] * u_pad[:, CONV_K - 1 - k: CONV_K - 1 - k + S] for k in range(CONV_K))
    o_conv = cb * y
    merged = jnp.concatenate([rmsnorm(o_nsa, gn_nsa), rmsnorm(o_conv, gn_conv)], axis=-1)
    return merged @ w_out


def memory_attention(h, mem_n, wq, wkv, wo):
    B, S, _ = h.shape
    M = mem_n.shape[1]
    q = (h @ wq).reshape(B, S, MEM_HEADS, MEM_HEAD_DIM)
    k, v = jnp.split(mem_n @ wkv, 2, axis=-1)
    k = k.reshape(B, M, MEM_HEADS, MEM_HEAD_DIM)
    v = v.reshape(B, M, MEM_HEADS, MEM_HEAD_DIM)
    lg = jnp.einsum('bshd,bmhd->bhsm', q, k).astype(jnp.float32) * (MEM_HEAD_DIM ** -0.5)
    p = jax.nn.softmax(lg, axis=-1)
    o = jnp.einsum('bhsm,bmhd->bshd', p, v.astype(jnp.float32)).reshape(B, S, MEM_WIDTH)
    return o.astype(h.dtype) @ wo


def swiglu(h, wg, wu, wd):
    return (jax.nn.silu(h @ wg) * (h @ wu)) @ wd


def setup_inputs(seed: int = 0) -> dict:
    key = jax.random.key(seed)
    ks = jax.random.split(key, 24)
    f32 = jnp.float32
    nrm = lambda k, shape, fan_in: jax.random.normal(k, shape, f32) * fan_in ** -0.5
    gain = lambda k, shape: 1.0 + 0.05 * jax.random.normal(k, shape, f32)
    return {
        "x": jax.random.normal(ks[0], (BATCH, SEQ, D_MODEL), f32),
        "mem": jax.random.normal(ks[1], (BATCH, MEM_LEN, D_MODEL), f32),
        "rel_bias": 0.3 * jax.random.normal(ks[2], (NSA_HEADS, N_BUCKETS), f32),
        "norms": gain(ks[3], (DEPTH, 6, D_MODEL)),
        "mem_norm": gain(ks[4], (DEPTH, D_MODEL)),
        "gn_nsa": gain(ks[5], (DEPTH, NSA_WIDTH)),
        "gn_conv": gain(ks[6], (DEPTH, CONV_WIDTH)),
        "w_in": nrm(ks[7], (DEPTH, D_MODEL, IN_WIDTH), D_MODEL),
        "w_out": nrm(ks[8], (DEPTH, MIX_WIDTH, D_MODEL), MIX_WIDTH),
        "cmp_pe": 0.1 * jax.random.normal(ks[9], (DEPTH, CMP_BLOCK, HEAD_DIM), f32),
        "cmp_k_w1": nrm(ks[10], (DEPTH, CMP_BLOCK * HEAD_DIM, CMP_HIDDEN), CMP_BLOCK * HEAD_DIM),
        "cmp_k_w2": nrm(ks[11], (DEPTH, CMP_HIDDEN, HEAD_DIM), CMP_HIDDEN),
        "cmp_v_w1": nrm(ks[12], (DEPTH, CMP_BLOCK * HEAD_DIM, CMP_HIDDEN), CMP_BLOCK * HEAD_DIM),
        "cmp_v_w2": nrm(ks[13], (DEPTH, CMP_HIDDEN, HEAD_DIM), CMP_HIDDEN),
        "conv_w": nrm(ks[14], (DEPTH, CONV_K, CONV_WIDTH), CONV_K),
        "mem_wq": nrm(ks[15], (DEPTH, D_MODEL, MEM_WIDTH), D_MODEL),
        "mem_wkv": nrm(ks[16], (DEPTH, D_MODEL, 2 * MEM_WIDTH), D_MODEL),
        "mem_wo": nrm(ks[17], (DEPTH, MEM_WIDTH, D_MODEL), MEM_WIDTH),
        "ffn_wg": nrm(ks[18], (DEPTH, D_MODEL, FFN_HIDDEN), D_MODEL),
        "ffn_wu": nrm(ks[19], (DEPTH, D_MODEL, FFN_HIDDEN), D_MODEL),
        "ffn_wd": nrm(ks[20], (DEPTH, FFN_HIDDEN, D_MODEL), FFN_HIDDEN),
    }


def reference(x, mem, rel_bias, norms, mem_norm, gn_nsa, gn_conv, w_in, w_out, cmp_pe,
              cmp_k_w1, cmp_k_w2, cmp_v_w1, cmp_v_w2, conv_w, mem_wq, mem_wkv, mem_wo,
              ffn_wg, ffn_wu, ffn_wd):
    for l in range(DEPTH):
        h = hybrid_mixer(rmsnorm(x, norms[l, 0]), w_in[l], w_out[l], gn_nsa[l], gn_conv[l], conv_w[l],
                         rel_bias, cmp_pe[l], cmp_k_w1[l], cmp_k_w2[l], cmp_v_w1[l], cmp_v_w2[l])
        x = x + rmsnorm(h, norms[l, 1])
        h = memory_attention(rmsnorm(x, norms[l, 2]), rmsnorm(mem, mem_norm[l]),
                             mem_wq[l], mem_wkv[l], mem_wo[l])
        x = x + rmsnorm(h, norms[l, 3])
        h = swiglu(rmsnorm(x, norms[l, 4]), ffn_wg[l], ffn_wu[l], ffn_wd[l])
        x = x + rmsnorm(h, norms[l, 5])
    return x
```

```python
import functools
import math

import numpy as np
import jax
import jax.numpy as jnp
from jax import lax
from jax.experimental import pallas as pl
from jax.experimental.pallas import tpu as pltpu

F32 = jnp.float32
BF16 = jnp.bfloat16

HEAD_DIM = 64
NSA_HEADS = 8
KV_GROUPS = 2
HPG = NSA_HEADS // KV_GROUPS
NSA_WIDTH = NSA_HEADS * HEAD_DIM
KV_WIDTH = KV_GROUPS * HEAD_DIM
CONV_WIDTH = 512
CONV_K = 3
CMP_STRIDE = 16
CMP_BLOCK = 32
CMP_HIDDEN = 128
SLC_BLOCK = 64
SLC_TOPK = 8
WINDOW = 256
FORCE_BONUS = 1.0e4
N_BUCKETS = 32
MAX_DISTANCE = 128
MEM_HEADS = 4
MEM_HEAD_DIM = 64
MEM_WIDTH = MEM_HEADS * MEM_HEAD_DIM
RMS_EPS = 1e-6
NEG_INF = -1e30
GATE_LANES = 128

TQ = 256
NT_DIMS = (((1,), (1,)), ((), ()))

VMEM_LIMIT = 56 * 1024 * 1024


def _rms(x, g):
    return x * lax.rsqrt(jnp.mean(x * x, axis=-1, keepdims=True) + RMS_EPS) * g


def _dot(a, b):
    return jnp.dot(a, b, preferred_element_type=F32)


def _dot_nt(a, b):
    return lax.dot_general(a, b, NT_DIMS, preferred_element_type=F32)


def _sigmoid(x):
    return 1.0 / (1.0 + jnp.exp(-x))


def _params(*sem):
    return pltpu.CompilerParams(dimension_semantics=sem, vmem_limit_bytes=VMEM_LIMIT)


def _const_spec(shape):
    return pl.BlockSpec(shape, lambda *_: (0,) * len(shape))


def _inproj_kernel(x_ref, g_ref, w_ref, main_ref, kc_ref, vc_ref, kvsw_ref, gl_ref):
    xn = _rms(x_ref[...], g_ref[...]).astype(BF16)

    def mm(lo, hi):
        return _dot(xn, w_ref[:, lo:hi])

    for c in range(4):
        main_ref[:, 512 * c:512 * (c + 1)] = mm(512 * c, 512 * (c + 1)).astype(BF16)
    r = mm(2048, 2304)
    kc_ref[...] = r[:, :KV_WIDTH].astype(BF16)
    vc_ref[...] = r[:, KV_WIDTH:].astype(BF16)
    kvsw_ref[...] = mm(2304, 2816).astype(BF16)
    gl_ref[...] = mm(2816, 2944)


def _inproj(x2, g, w, tm):
    t, d = x2.shape
    nw = w.shape[1]
    return pl.pallas_call(
        _inproj_kernel,
        grid=(t // tm,),
        in_specs=[pl.BlockSpec((tm, d), lambda i: (i, 0)),
                  _const_spec((1, d)),
                  _const_spec((d, nw))],
        out_specs=[pl.BlockSpec((tm, 2048), lambda i: (i, 0)),
                   pl.BlockSpec((tm, KV_WIDTH), lambda i: (i, 0)),
                   pl.BlockSpec((tm, KV_WIDTH), lambda i: (i, 0)),
                   pl.BlockSpec((tm, 4 * KV_WIDTH), lambda i: (i, 0)),
                   pl.BlockSpec((tm, GATE_LANES), lambda i: (i, 0))],
        out_shape=[jax.ShapeDtypeStruct((t, 2048), BF16),
                   jax.ShapeDtypeStruct((t, KV_WIDTH), BF16),
                   jax.ShapeDtypeStruct((t, KV_WIDTH), BF16),
                   jax.ShapeDtypeStruct((t, 4 * KV_WIDTH), BF16),
                   jax.ShapeDtypeStruct((t, GATE_LANES), F32)],
        compiler_params=_params("parallel"),
        name="inproj",
    )(x2, g, w)


def _norm_matmul_kernel(x_ref, g_ref, w_ref, o_ref):
    xn = _rms(x_ref[...], g_ref[...]).astype(BF16)
    o_ref[...] = _dot(xn, w_ref[...]).astype(o_ref.dtype)


def _norm_matmul(x2, g, w, tm, out_dtype, name):
    t, d = x2.shape
    n = w.shape[1]
    return pl.pallas_call(
        _norm_matmul_kernel,
        grid=(t // tm,),
        in_specs=[pl.BlockSpec((tm, d), lambda i: (i, 0)),
                  _const_spec((1, d)),
                  _const_spec((d, n))],
        out_specs=pl.BlockSpec((tm, n), lambda i: (i, 0)),
        out_shape=jax.ShapeDtypeStruct((t, n), out_dtype),
        compiler_params=_params("parallel"),
        name=name,
    )(x2, g, w)


def _gelu_tanh(x):
    c = math.sqrt(2.0 / math.pi)
    return x * (0.5 * (1.0 + jnp.tanh(c * (x + 0.044715 * (x * x * x)))))


def _compress_kernel(kc_ref, vc_ref, pe_ref, wka_ref, wkb_ref, wk2_ref, wva_ref, wvb_ref, wv2_ref,
                     ko_ref, vo_ref, *, nch, ncp):
    row = lax.broadcasted_iota(jnp.int32, (nch, KV_WIDTH), 0)

    def one(x_ref, wa_ref, wb_ref, w2_ref, o_ref):
        x = x_ref[0].astype(F32)
        a = _dot((x + pe_ref[0:1, :]).astype(BF16), wa_ref[...])
        b = _dot((x + pe_ref[1:2, :]).astype(BF16), wb_ref[...])
        hid = _gelu_tanh(a + pltpu.roll(b, nch - 1, 0))
        out = _dot(hid.astype(BF16), w2_ref[...])
        out = jnp.where(row < nch - 1, out, 0.0).astype(BF16)
        if ncp > nch:
            out = jnp.concatenate([out, jnp.zeros((ncp - nch, KV_WIDTH), BF16)], axis=0)
        o_ref[0] = out

    one(kc_ref, wka_ref, wkb_ref, wk2_ref, ko_ref)
    one(vc_ref, wva_ref, wvb_ref, wv2_ref, vo_ref)


def _compress(kc3, vc3, pe2, wka, wkb, wk2, wva, wvb, wv2, ncp):
    b, nch, cw = kc3.shape
    hid2 = wka.shape[1]
    xspec = pl.BlockSpec((1, nch, cw), lambda i: (i, 0, 0))
    ospec = pl.BlockSpec((1, ncp, KV_WIDTH), lambda i: (i, 0, 0))
    return pl.pallas_call(
        functools.partial(_compress_kernel, nch=nch, ncp=ncp),
        grid=(b,),
        in_specs=[xspec, xspec, _const_spec((2, cw)),
                  _const_spec((cw, hid2)), _const_spec((cw, hid2)), _const_spec((hid2, KV_WIDTH)),
                  _const_spec((cw, hid2)), _const_spec((cw, hid2)), _const_spec((hid2, KV_WIDTH))],
        out_specs=[ospec, ospec],
        out_shape=[jax.ShapeDtypeStruct((b, ncp, KV_WIDTH), BF16)] * 2,
        compiler_params=_params("parallel"),
        name="compress",
    )(kc3, vc3, pe2, wka, wkb, wk2, wva, wvb, wv2)


def _nsa_kernel(rb_ref, q_ref, gl_ref, kc_ref, vc_ref, ks_ref, vs_ref, kw_ref, vw_ref,
                bc_ref, nt_ref, ovt_ref, e_ref, eye_ref, o_ref, *, ncp, nbp):
    i = pl.program_id(1)
    t0 = i * TQ
    rows = HPG * TQ
    jprev = jnp.maximum(i - 1, 0)
    has_prev = i > 0

    lane_q = lax.broadcasted_iota(jnp.int32, (TQ, KV_WIDTH), 1)
    r_idx = lax.broadcasted_iota(jnp.int32, (rows, TQ), 0) & (TQ - 1)
    c_idx = lax.broadcasted_iota(jnp.int32, (rows, TQ), 1)
    causal = r_idx >= c_idx
    prev_win = jnp.logical_and(r_idx < c_idx, has_prev)

    rc = lax.broadcasted_iota(jnp.int32, (rows, ncp), 0) & (TQ - 1)
    nc = lax.broadcasted_iota(jnp.int32, (rows, ncp), 1)
    mask_c = (t0 + rc - CMP_STRIDE * nc - (CMP_BLOCK - 1)) >= 0

    blk = lax.broadcasted_iota(jnp.int32, (nbp, TQ), 0)
    cur = (t0 + lax.broadcasted_iota(jnp.int32, (nbp, TQ), 1)) >> 6
    valid = blk <= cur
    forced = (blk == 0) | (blk == cur) | (blk == cur - 1)
    bonus = jnp.where(forced, FORCE_BONUS, 0.0).astype(F32)

    gates = _sigmoid(gl_ref[...])

    def ktile(ref, j):
        return ref[0, pl.ds(pl.multiple_of(j * TQ, TQ), TQ), :]

    outs = []
    for g in range(KV_GROUPS):
        keep = (lane_q < HEAD_DIM) if g == 0 else (lane_q >= HEAD_DIM)
        qg = jnp.concatenate(
            [jnp.where(keep, q_ref[:, KV_WIDTH * p:KV_WIDTH * (p + 1)], jnp.zeros((), BF16))
             for p in range(HPG)], axis=0)

        def gate_col(k):
            return jnp.concatenate(
                [gates[:, (g * HPG + p) * 3 + k:(g * HPG + p) * 3 + k + 1] for p in range(HPG)], axis=0)

        lg = _dot_nt(qg, kc_ref[0]) + bc_ref[g].reshape(rows, ncp)
        lg = jnp.where(mask_c, lg, NEG_INF)
        m = jnp.max(lg, axis=-1, keepdims=True)
        e = jnp.where(mask_c, jnp.exp(lg - m), 0.0)
        p_cmp = e / jnp.maximum(jnp.sum(e, axis=-1, keepdims=True), 1e-30)
        out_g = gate_col(0) * _dot(p_cmp.astype(BF16), vc_ref[0])

        psum = p_cmp[0:TQ]
        for p in range(1, HPG):
            psum = psum + p_cmp[p * TQ:(p + 1) * TQ]
        hi = psum.astype(BF16)
        r1 = psum - hi.astype(F32)
        mid = r1.astype(BF16)
        lo = (r1 - mid.astype(F32)).astype(BF16)
        ovt = ovt_ref[...]
        imp_t = _dot_nt(ovt, hi) + _dot_nt(ovt, mid) + _dot_nt(ovt, lo)

        score = jnp.where(valid, imp_t + bonus, NEG_INF)
        cnt = jnp.zeros((nbp, TQ), jnp.int32)
        for ii in range(nbp):
            si = score[ii:ii + 1, :]
            beats = (si > score) | ((si == score) & (blk > ii))
            cnt = cnt + beats.astype(jnp.int32)
        sel = valid & (cnt < SLC_TOPK)
        mneg_t = jnp.where(sel, 0.0, NEG_INF).astype(BF16)
        mneg_t = jnp.concatenate([mneg_t, jnp.zeros((KV_WIDTH - nbp, TQ), BF16)], axis=0)
        mneg = _dot_nt(eye_ref[...], mneg_t).astype(BF16)

        def sel_bias(j):
            mb = _dot(mneg, e_ref[j])
            return jnp.concatenate([mb] * HPG, axis=0)

        far_bias = jnp.concatenate(
            [jnp.full((TQ, 1), rb_ref[g * HPG + p], F32) for p in range(HPG)], axis=0)
        nt_prev = nt_ref[g, :, 0:TQ]
        nt_diag = nt_ref[g, :, TQ:2 * TQ]

        s = _dot_nt(qg, ktile(ks_ref, i)) + nt_diag + sel_bias(i)
        s = jnp.where(causal, s, NEG_INF)
        m = jnp.max(s, axis=-1, keepdims=True)
        e = jnp.exp(s - m)
        l = jnp.sum(e, axis=-1, keepdims=True)
        acc = _dot(e.astype(BF16), ktile(vs_ref, i))

        def update(carry, s, v):
            m, l, acc = carry
            m_new = jnp.maximum(m, jnp.max(s, axis=-1, keepdims=True))
            alpha = jnp.exp(m - m_new)
            e = jnp.exp(s - m_new)
            l = alpha * l + jnp.sum(e, axis=-1, keepdims=True)
            acc = alpha * acc + _dot(e.astype(BF16), v)
            return m_new, l, acc

        s = _dot_nt(qg, ktile(ks_ref, jprev)) + nt_prev + sel_bias(jprev)
        s = jnp.where(has_prev, s, NEG_INF)
        carry = update((m, l, acc), s, ktile(vs_ref, jprev))

        def far_body(j, carry):
            s = _dot_nt(qg, ktile(ks_ref, j)) + far_bias + sel_bias(j)
            return update(carry, s, ktile(vs_ref, j))

        m, l, acc = lax.fori_loop(0, jnp.maximum(i - 1, 0), far_body, carry)
        out_g = out_g + (gate_col(1) / l) * acc

        s1 = jnp.where(causal, _dot_nt(qg, ktile(kw_ref, i)) + nt_diag, NEG_INF)
        s0 = jnp.where(prev_win, _dot_nt(qg, ktile(kw_ref, jprev)) + nt_prev, NEG_INF)
        m = jnp.maximum(jnp.max(s1, axis=-1, keepdims=True), jnp.max(s0, axis=-1, keepdims=True))
        e1 = jnp.exp(s1 - m)
        e0 = jnp.exp(s0 - m)
        l = jnp.sum(e1, axis=-1, keepdims=True) + jnp.sum(e0, axis=-1, keepdims=True)
        acc = _dot(e1.astype(BF16), ktile(vw_ref, i)) + _dot(e0.astype(BF16), ktile(vw_ref, jprev))
        out_g = out_g + (gate_col(2) / l) * acc
        outs.append(out_g)

    for p in range(HPG):
        o_ref[:, KV_WIDTH * p:KV_WIDTH * (p + 1)] = jnp.where(
            lane_q < HEAD_DIM, outs[0][p * TQ:(p + 1) * TQ], outs[1][p * TQ:(p + 1) * TQ]).astype(BF16)


def _nsa(rb_far, main, gl, kcmp, vcmp, kvsw, bias_c, near, ovt, emat, eye, b, s):
    ncp = kcmp.shape[1]
    nbp = ovt.shape[0]
    nq = s // TQ
    kv_spec = lambda col: pl.BlockSpec((1, s, KV_WIDTH), lambda bi, i: (bi, 0, col))
    kvsw3 = kvsw.reshape(b, s, 4 * KV_WIDTH)
    return pl.pallas_call(
        functools.partial(_nsa_kernel, ncp=ncp, nbp=nbp),
        grid=(b, nq),
        in_specs=[pl.BlockSpec(memory_space=pltpu.SMEM),
                  pl.BlockSpec((TQ, NSA_WIDTH), lambda bi, i: (bi * nq + i, 0)),
                  pl.BlockSpec((TQ, GATE_LANES), lambda bi, i: (bi * nq + i, 0)),
                  pl.BlockSpec((1, ncp, KV_WIDTH), lambda bi, i: (bi, 0, 0)),
                  pl.BlockSpec((1, ncp, KV_WIDTH), lambda bi, i: (bi, 0, 0)),
                  kv_spec(0), kv_spec(1), kv_spec(2), kv_spec(3),
                  pl.BlockSpec((KV_GROUPS, HPG, TQ, ncp), lambda bi, i: (0, 0, i, 0)),
                  _const_spec((KV_GROUPS, HPG * TQ, 2 * TQ)),
                  _const_spec((nbp, ncp)),
                  _const_spec((nq, KV_WIDTH, TQ)),
                  _const_spec((TQ, TQ))],
        out_specs=pl.BlockSpec((TQ, NSA_WIDTH), lambda bi, i: (bi * nq + i, 0)),
        out_shape=jax.ShapeDtypeStruct((b * s, NSA_WIDTH), BF16),
        compiler_params=_params("parallel", "arbitrary"),
        name="nsa_attention",
    )(rb_far, main, gl, kcmp, vcmp, kvsw3, kvsw3, kvsw3, kvsw3, bias_c, near, ovt, emat, eye)


def _mixout_kernel(x_ref, on_ref, cb_ref, cc_ref, cx_ref, hc_ref, hx_ref, cw_ref, gnn_ref, gnc_ref,
                   wo_ref, g1_ref, out_ref, *, tiles_per_seq, tm):
    first = (pl.program_id(0) % tiles_per_seq) == 0
    u = cc_ref[...].astype(F32) * cx_ref[...].astype(F32)
    halo = hc_ref[...].astype(F32) * hx_ref[...].astype(F32)
    halo = jnp.where(first, 0.0, halo)
    ue = jnp.concatenate([halo, u], axis=0)
    u1 = pltpu.roll(ue, 1, 0)[8:]
    u2 = pltpu.roll(ue, 2, 0)[8:]
    y = cw_ref[0:1, :] * u + cw_ref[1:2, :] * u1 + cw_ref[2:3, :] * u2
    o_conv = cb_ref[...].astype(F32) * y
    merged = jnp.concatenate([_rms(on_ref[...].astype(F32), gnn_ref[...]),
                              _rms(o_conv, gnc_ref[...])], axis=-1).astype(BF16)
    h = _dot(merged, wo_ref[...])
    out_ref[...] = x_ref[...] + _rms(h, g1_ref[...])


def _mixout(x2, o_nsa, main, cw, gnn, gnc, wo, g1, s, tm):
    t, d = x2.shape
    tps = s // tm
    hb = tm // 8
    halo_spec = lambda col: pl.BlockSpec((8, CONV_WIDTH), lambda i: (jnp.maximum(i * hb - 1, 0), col))
    return pl.pallas_call(
        functools.partial(_mixout_kernel, tiles_per_seq=tps, tm=tm),
        grid=(t // tm,),
        in_specs=[pl.BlockSpec((tm, d), lambda i: (i, 0)),
                  pl.BlockSpec((tm, NSA_WIDTH), lambda i: (i, 0)),
                  pl.BlockSpec((tm, CONV_WIDTH), lambda i: (i, 1)),
                  pl.BlockSpec((tm, CONV_WIDTH), lambda i: (i, 2)),
                  pl.BlockSpec((tm, CONV_WIDTH), lambda i: (i, 3)),
                  halo_spec(2), halo_spec(3),
                  _const_spec((8, CONV_WIDTH)),
                  _const_spec((1, NSA_WIDTH)), _const_spec((1, CONV_WIDTH)),
                  _const_spec((NSA_WIDTH + CONV_WIDTH, d)),
                  _const_spec((1, d))],
        out_specs=pl.BlockSpec((tm, d), lambda i: (i, 0)),
        out_shape=jax.ShapeDtypeStruct((t, d), F32),
        compiler_params=_params("parallel"),
        name="mix_out",
    )(x2, o_nsa, main, main, main, main, main, cw, gnn, gnc, wo, g1)


def _memattn_kernel(x_ref, g2_ref, wq_ref, kv_ref, wo_ref, g3_ref, out_ref, *, tm):
    x = x_ref[...]
    q = _dot(_rms(x, g2_ref[...]).astype(BF16), wq_ref[...]).astype(BF16)
    lane = lax.broadcasted_iota(jnp.int32, (tm, 2 * MEM_HEAD_DIM), 1)
    lo_half = lane < MEM_HEAD_DIM
    zero = jnp.zeros((), BF16)
    slots = []
    for sl in range(MEM_HEADS // 2):
        qs = q[:, 128 * sl:128 * (sl + 1)]
        q2 = jnp.concatenate([jnp.where(lo_half, qs, zero), jnp.where(lo_half, zero, qs)], axis=0)
        k = kv_ref[0, :, 128 * sl:128 * (sl + 1)]
        v = kv_ref[0, :, MEM_WIDTH + 128 * sl:MEM_WIDTH + 128 * (sl + 1)]
        lg = _dot_nt(q2, k)
        e = jnp.exp(lg - jnp.max(lg, axis=-1, keepdims=True))
        p = e / jnp.sum(e, axis=-1, keepdims=True)
        o2 = _dot(p.astype(BF16), v)
        slots.append(jnp.where(lo_half, o2[:tm], o2[tm:]))
    o = jnp.concatenate(slots, axis=-1).astype(BF16)
    h = _dot(o, wo_ref[...])
    out_ref[...] = x + _rms(h, g3_ref[...])


def _memattn(x2, g2, wq, memkv, wo, g3, s, tm):
    t, d = x2.shape
    tps = s // tm
    ml = memkv.shape[1]
    return pl.pallas_call(
        functools.partial(_memattn_kernel, tm=tm),
        grid=(t // tm,),
        in_specs=[pl.BlockSpec((tm, d), lambda i: (i, 0)),
                  _const_spec((1, d)),
                  _const_spec((d, MEM_WIDTH)),
                  pl.BlockSpec((1, ml, 2 * MEM_WIDTH), lambda i: (i // tps, 0, 0)),
                  _const_spec((MEM_WIDTH, d)),
                  _const_spec((1, d))],
        out_specs=pl.BlockSpec((tm, d), lambda i: (i, 0)),
        out_shape=jax.ShapeDtypeStruct((t, d), F32),
        compiler_params=_params("parallel"),
        name="mem_attention",
    )(x2, g2, wq, memkv, wo, g3)


def _ffn_kernel(x_ref, g4_ref, wg_ref, wu_ref, wd_ref, g5_ref, out_ref, xn_ref, acc_ref):
    j = pl.program_id(1)

    @pl.when(j == 0)
    def _():
        xn_ref[...] = _rms(x_ref[...], g4_ref[...]).astype(BF16)

    xn = xn_ref[...]
    a = _dot(xn, wg_ref[...])
    u = _dot(xn, wu_ref[...])
    hid = (a * _sigmoid(a) * u).astype(BF16)
    part = _dot(hid, wd_ref[...])

    @pl.when(j == 0)
    def _():
        acc_ref[...] = part

    @pl.when(j > 0)
    def _():
        acc_ref[...] += part

    @pl.when(j == pl.num_programs(1) - 1)
    def _():
        out_ref[...] = x_ref[...] + _rms(acc_ref[...], g5_ref[...])


def _ffn(x2, g4, wg, wu, wd, g5, tm, th):
    t, d = x2.shape
    hdim = wg.shape[1]
    return pl.pallas_call(
        _ffn_kernel,
        grid=(t // tm, hdim // th),
        in_specs=[pl.BlockSpec((tm, d), lambda i, j: (i, 0)),
                  _const_spec((1, d)),
                  pl.BlockSpec((d, th), lambda i, j: (0, j)),
                  pl.BlockSpec((d, th), lambda i, j: (0, j)),
                  pl.BlockSpec((th, d), lambda i, j: (j, 0)),
                  _const_spec((1, d))],
        out_specs=pl.BlockSpec((tm, d), lambda i, j: (i, 0)),
        out_shape=jax.ShapeDtypeStruct((t, d), F32),
        scratch_shapes=[pltpu.VMEM((tm, d), BF16), pltpu.VMEM((tm, d), F32)],
        compiler_params=_params("parallel", "arbitrary"),
        name="swiglu",
    )(x2, g4, wg, wu, wd, g5)


def _t5_bucket(dist):
    n = jnp.maximum(dist, 0)
    max_exact = N_BUCKETS // 2
    nf = jnp.maximum(n, 1).astype(F32)
    large = max_exact + (jnp.log(nf / max_exact) / math.log(MAX_DISTANCE / max_exact)
                         * (N_BUCKETS - max_exact)).astype(jnp.int32)
    large = jnp.minimum(large, N_BUCKETS - 1)
    return jnp.where(n < max_exact, n, large)


def _pair_heads(a, axis):
    chunks = jnp.split(a, NSA_HEADS, axis=axis)
    order = []
    for p in range(HPG):
        order += [chunks[p], chunks[HPG + p]]
    return jnp.concatenate(order, axis=axis)


def _blockdiag_first(w1):
    hdim = w1.shape[1]
    w1r = w1.reshape(CMP_BLOCK, HEAD_DIM, hdim)
    eye = jnp.eye(KV_GROUPS, dtype=w1.dtype)
    halves = []
    for part in (w1r[:CMP_STRIDE], w1r[CMP_STRIDE:]):
        halves.append(jnp.einsum('jdm,gh->jgdhm', part, eye)
                      .reshape(CMP_STRIDE * KV_WIDTH, KV_GROUPS * hdim).astype(BF16))
    return halves


def _blockdiag_second(w2):
    eye = jnp.eye(KV_GROUPS, dtype=w2.dtype)
    return jnp.einsum('md,gh->gmhd', w2, eye).reshape(KV_GROUPS * w2.shape[0], KV_WIDTH).astype(BF16)


def kernel(x, mem, rel_bias, norms, mem_norm, gn_nsa, gn_conv, w_in, w_out, cmp_pe, cmp_k_w1, cmp_k_w2,
           cmp_v_w1, cmp_v_w2, conv_w, mem_wq, mem_wkv, mem_wo, ffn_wg, ffn_wu, ffn_wd):
    b, s, d = x.shape
    depth = w_in.shape[0]
    ml = mem.shape[1]
    t = b * s
    assert s % TQ == 0 and s // SLC_BLOCK <= 32
    nch = s // CMP_STRIDE
    ncp = -(-nch // 128) * 128
    n_cmp = nch - 1
    n_slc = s // SLC_BLOCK
    nbp = 32
    tm = 512

    r = jnp.arange(TQ)[:, None]
    c = jnp.arange(2 * TQ)[None, :]
    near = rel_bias[:, _t5_bucket(r + TQ - c)].reshape(KV_GROUPS, HPG * TQ, 2 * TQ)
    tpos = jnp.arange(s)[:, None]
    npos = jnp.arange(ncp)[None, :]
    bias_c = rel_bias[:, _t5_bucket(tpos - CMP_STRIDE * npos - (CMP_BLOCK - 1))]
    bias_c = bias_c.reshape(KV_GROUPS, HPG, s, ncp)
    rb_far = rel_bias[:, N_BUCKETS - 1]
    cs = np.arange(ncp) * CMP_STRIDE
    ss = np.arange(nbp) * SLC_BLOCK
    ov = np.clip(np.minimum(cs[:, None] + CMP_BLOCK, ss[None, :] + SLC_BLOCK)
                 - np.maximum(cs[:, None], ss[None, :]), 0, None) / CMP_BLOCK
    ov[n_cmp:, :] = 0.0
    ov[:, n_slc:] = 0.0
    ovt = jnp.asarray(ov.T, dtype=BF16)
    key_blk = (np.arange(s) // SLC_BLOCK).reshape(s // TQ, 1, TQ)
    emat = jnp.asarray(np.arange(KV_WIDTH)[None, :, None] == key_blk, dtype=BF16)
    eye = jnp.eye(TQ, dtype=BF16)

    x2 = x.reshape(t, d)
    mem2 = mem.reshape(b * ml, d)
    for l in range(depth):
        wq, wkc, wvc, wks, wvs, wkw, wvw, wgl, wcb, wcc, wcx = jnp.split(
            w_in[l], np.cumsum((NSA_WIDTH,) + (KV_WIDTH,) * 6 + (3 * NSA_HEADS,) + (CONV_WIDTH,) * 2), axis=1)
        wgl = jnp.pad(wgl, ((0, 0), (0, GATE_LANES - wgl.shape[1])))
        w_all = jnp.concatenate([_pair_heads(wq, 1) * (HEAD_DIM ** -0.5), wcb, wcc, wcx,
                                 wkc, wvc, wks, wvs, wkw, wvw, wgl], axis=1).astype(BF16)
        gnn = _pair_heads(gn_nsa[l], 0)[None, :]
        wo = jnp.concatenate([_pair_heads(w_out[l][:NSA_WIDTH], 0), w_out[l][NSA_WIDTH:]], axis=0).astype(BF16)
        pe = cmp_pe[l].reshape(2, CMP_STRIDE, 1, HEAD_DIM)
        pe2 = jnp.broadcast_to(pe, (2, CMP_STRIDE, KV_GROUPS, HEAD_DIM)).reshape(2, CMP_STRIDE * KV_WIDTH)
        wka, wkb = _blockdiag_first(cmp_k_w1[l])
        wva, wvb = _blockdiag_first(cmp_v_w1[l])
        cw = jnp.pad(conv_w[l], ((0, 8 - CONV_K), (0, 0)))

        main, kc, vc, kvsw, gl = _inproj(x2, norms[l, 0][None, :], w_all, tm)
        kcmp, vcmp = _compress(kc.reshape(b, nch, CMP_STRIDE * KV_WIDTH),
                               vc.reshape(b, nch, CMP_STRIDE * KV_WIDTH), pe2,
                               wka, wkb, _blockdiag_second(cmp_k_w2[l]),
                               wva, wvb, _blockdiag_second(cmp_v_w2[l]), ncp)
        o_nsa = _nsa(rb_far, main, gl, kcmp, vcmp, kvsw, bias_c, near, ovt, emat, eye, b, s)
        x2 = _mixout(x2, o_nsa, main, cw, gnn, gn_conv[l][None, :], wo, norms[l, 1][None, :], s, tm)

        memkv = _norm_matmul(mem2, mem_norm[l][None, :], mem_wkv[l].astype(BF16), ml, BF16, "mem_kv")
        x2 = _memattn(x2, norms[l, 2][None, :], (mem_wq[l] * (MEM_HEAD_DIM ** -0.5)).astype(BF16),
                      memkv.reshape(b, ml, 2 * MEM_WIDTH), mem_wo[l].astype(BF16), norms[l, 3][None, :], s, tm)

        x2 = _ffn(x2, norms[l, 4][None, :], ffn_wg[l].astype(BF16), ffn_wu[l].astype(BF16),
                  ffn_wd[l].astype(BF16), norms[l, 5][None, :], 1024 if t % 1024 == 0 else tm, 256)
    return x2.reshape(b, s, d)
```

```python
import functools
import math

import numpy as np
import jax
import jax.numpy as jnp
from jax import lax
from jax.experimental import pallas as pl
from jax.experimental.pallas import tpu as pltpu

F32 = jnp.float32
BF16 = jnp.bfloat16

HEAD_DIM = 64
NSA_HEADS = 8
KV_GROUPS = 2
HPG = NSA_HEADS // KV_GROUPS
NSA_WIDTH = NSA_HEADS * HEAD_DIM
KV_WIDTH = KV_GROUPS * HEAD_DIM
CONV_WIDTH = 512
CONV_K = 3
CMP_STRIDE = 16
CMP_BLOCK = 32
CMP_HIDDEN = 128
SLC_BLOCK = 64
SLC_TOPK = 8
WINDOW = 256
FORCE_BONUS = 1.0e4
N_BUCKETS = 32
MAX_DISTANCE = 128
MEM_HEADS = 4
MEM_HEAD_DIM = 64
MEM_WIDTH = MEM_HEADS * MEM_HEAD_DIM
RMS_EPS = 1e-6
NEG_INF = -1e30
GATE_LANES = 128

TQ = 256
RC = 64
NT_DIMS = (((1,), (1,)), ((), ()))

VMEM_LIMIT = 56 * 1024 * 1024


def _rms(x, g):
    return x * lax.rsqrt(jnp.mean(x * x, axis=-1, keepdims=True) + RMS_EPS) * g


def _dot(a, b):
    return jnp.dot(a, b, preferred_element_type=F32)


def _dot_nt(a, b):
    return lax.dot_general(a, b, NT_DIMS, preferred_element_type=F32)


def _sigmoid(x):
    return 1.0 / (1.0 + jnp.exp(-x))


def _params(*sem):
    return pltpu.CompilerParams(dimension_semantics=sem, vmem_limit_bytes=VMEM_LIMIT)


def _const_spec(shape):
    return pl.BlockSpec(shape, lambda *_: (0,) * len(shape))


def _inproj_kernel(x_ref, g_ref, w_ref, main_ref, kc_ref, vc_ref, kvsw_ref, gl_ref):
    xn = _rms(x_ref[...], g_ref[...]).astype(BF16)

    def mm(lo, hi):
        return _dot(xn, w_ref[:, lo:hi])

    for c in range(4):
        main_ref[:, 512 * c:512 * (c + 1)] = mm(512 * c, 512 * (c + 1)).astype(BF16)
    r = mm(2048, 2304)
    kc_ref[...] = r[:, :KV_WIDTH].astype(BF16)
    vc_ref[...] = r[:, KV_WIDTH:].astype(BF16)
    kvsw_ref[...] = mm(2304, 2816).astype(BF16)
    gl_ref[...] = mm(2816, 2944)


def _inproj(x2, g, w, tm):
    t, d = x2.shape
    nw = w.shape[1]
    return pl.pallas_call(
        _inproj_kernel,
        grid=(t // tm,),
        in_specs=[pl.BlockSpec((tm, d), lambda i: (i, 0)),
                  _const_spec((1, d)),
                  _const_spec((d, nw))],
        out_specs=[pl.BlockSpec((tm, 2048), lambda i: (i, 0)),
                   pl.BlockSpec((tm, KV_WIDTH), lambda i: (i, 0)),
                   pl.BlockSpec((tm, KV_WIDTH), lambda i: (i, 0)),
                   pl.BlockSpec((tm, 4 * KV_WIDTH), lambda i: (i, 0)),
                   pl.BlockSpec((tm, GATE_LANES), lambda i: (i, 0))],
        out_shape=[jax.ShapeDtypeStruct((t, 2048), BF16),
                   jax.ShapeDtypeStruct((t, KV_WIDTH), BF16),
                   jax.ShapeDtypeStruct((t, KV_WIDTH), BF16),
                   jax.ShapeDtypeStruct((t, 4 * KV_WIDTH), BF16),
                   jax.ShapeDtypeStruct((t, GATE_LANES), F32)],
        compiler_params=_params("parallel"),
        name="inproj",
    )(x2, g, w)


def _norm_matmul_kernel(x_ref, g_ref, w_ref, o_ref):
    xn = _rms(x_ref[...], g_ref[...]).astype(BF16)
    o_ref[...] = _dot(xn, w_ref[...]).astype(o_ref.dtype)


def _norm_matmul(x2, g, w, tm, out_dtype, name):
    t, d = x2.shape
    n = w.shape[1]
    return pl.pallas_call(
        _norm_matmul_kernel,
        grid=(t // tm,),
        in_specs=[pl.BlockSpec((tm, d), lambda i: (i, 0)),
                  _const_spec((1, d)),
                  _const_spec((d, n))],
        out_specs=pl.BlockSpec((tm, n), lambda i: (i, 0)),
        out_shape=jax.ShapeDtypeStruct((t, n), out_dtype),
        compiler_params=_params("parallel"),
        name=name,
    )(x2, g, w)


def _gelu_tanh(x):
    c = math.sqrt(2.0 / math.pi)
    return x * (0.5 * (1.0 + jnp.tanh(c * (x + 0.044715 * (x * x * x)))))


def _compress_kernel(kc_ref, vc_ref, pe_ref, wka_ref, wkb_ref, wk2_ref, wva_ref, wvb_ref, wv2_ref,
                     ko_ref, vo_ref, *, nch, ncp):
    row = lax.broadcasted_iota(jnp.int32, (nch, KV_WIDTH), 0)

    def one(x_ref, wa_ref, wb_ref, w2_ref, o_ref):
        x = x_ref[0].astype(F32)
        a = _dot((x + pe_ref[0:1, :]).astype(BF16), wa_ref[...])
        b = _dot((x + pe_ref[1:2, :]).astype(BF16), wb_ref[...])
        hid = _gelu_tanh(a + pltpu.roll(b, nch - 1, 0))
        out = _dot(hid.astype(BF16), w2_ref[...])
        out = jnp.where(row < nch - 1, out, 0.0).astype(BF16)
        if ncp > nch:
            out = jnp.concatenate([out, jnp.zeros((ncp - nch, KV_WIDTH), BF16)], axis=0)
        o_ref[0] = out

    one(kc_ref, wka_ref, wkb_ref, wk2_ref, ko_ref)
    one(vc_ref, wva_ref, wvb_ref, wv2_ref, vo_ref)


def _compress(kc3, vc3, pe2, wka, wkb, wk2, wva, wvb, wv2, ncp):
    b, nch, cw = kc3.shape
    hid2 = wka.shape[1]
    xspec = pl.BlockSpec((1, nch, cw), lambda i: (i, 0, 0))
    ospec = pl.BlockSpec((1, ncp, KV_WIDTH), lambda i: (i, 0, 0))
    return pl.pallas_call(
        functools.partial(_compress_kernel, nch=nch, ncp=ncp),
        grid=(b,),
        in_specs=[xspec, xspec, _const_spec((2, cw)),
                  _const_spec((cw, hid2)), _const_spec((cw, hid2)), _const_spec((hid2, KV_WIDTH)),
                  _const_spec((cw, hid2)), _const_spec((cw, hid2)), _const_spec((hid2, KV_WIDTH))],
        out_specs=[ospec, ospec],
        out_shape=[jax.ShapeDtypeStruct((b, ncp, KV_WIDTH), BF16)] * 2,
        compiler_params=_params("parallel"),
        name="compress",
    )(kc3, vc3, pe2, wka, wkb, wk2, wva, wvb, wv2)


def _nsa_kernel(rb_ref, q_ref, gl_ref, kc_ref, vc_ref, ks_ref, vs_ref, kw_ref, vw_ref,
                bc_ref, nt_ref, ovt_ref, et_ref, eye_ref, o_ref,
                qa_scr, s_scr, p_scr, m_scr, a_scr, acc_scr, out_scr, *, ncp, nbp):
    i = pl.program_id(1)
    t0 = i * TQ
    rows = HPG * TQ
    jprev = jnp.maximum(i - 1, 0)
    n_chunks = rows // RC

    lane = lax.broadcasted_iota(jnp.int32, (TQ, KV_WIDTH), 1)
    lo_half = lane < HEAD_DIM

    blk = lax.broadcasted_iota(jnp.int32, (nbp, TQ), 0)
    cur = (t0 + lax.broadcasted_iota(jnp.int32, (nbp, TQ), 1)) >> 6
    valid = blk <= cur
    forced = (blk == 0) | (blk == cur) | (blk == cur - 1)
    bonus = jnp.where(forced, FORCE_BONUS, 0.0).astype(F32)

    gates = _sigmoid(gl_ref[...])

    def ktile(ref, j):
        return ref[0, pl.ds(pl.multiple_of(j * TQ, TQ), TQ), :]

    def attend(g, q_lhs, k_tile, v_tile, first, table_off=None):
        keep = lo_half if g == 0 else jnp.logical_not(lo_half)
        v_aug = jnp.where(keep, v_tile, jnp.ones((), BF16))
        s_scr[...] = _dot_nt(q_lhs, k_tile)
        for c in range(n_chunks):
            sl = slice(c * RC, (c + 1) * RC)
            s = s_scr[sl, :]
            if table_off is not None:
                s = s + nt_ref[g, sl, table_off:table_off + TQ]
            m_loc = jnp.broadcast_to(jnp.max(s, axis=1, keepdims=True), (RC, KV_WIDTH))
            if table_off is None:
                far = rb_ref[g * HPG + (c * RC) // TQ]
                m_loc = m_loc + far
            if first:
                m_new = m_loc
            else:
                m_old = m_scr[sl, :]
                m_new = jnp.maximum(m_old, m_loc)
                a_scr[sl, :] = jnp.exp(m_old - m_new)
            m_scr[sl, :] = m_new
            shift = -m_new if table_off is not None else far - m_new
            e = jnp.exp(s + jnp.concatenate([shift, shift], axis=1))
            p_scr[sl, :] = e.astype(BF16)
        pv = _dot(p_scr[...], v_aug)
        if first:
            acc_scr[...] = pv
        else:
            acc_scr[...] = a_scr[...] * acc_scr[...] + pv

    def gate_rows(g, k):
        return jnp.concatenate(
            [jnp.broadcast_to(gates[:, (g * HPG + p) * 3 + k:(g * HPG + p) * 3 + k + 1], (TQ, KV_WIDTH))
             for p in range(HPG)], axis=0)

    def finish(g, k):
        acc = acc_scr[...]
        denom = pltpu.roll(acc, HEAD_DIM, 1)
        out_scr[g] = out_scr[g] + gate_rows(g, k) * (acc / denom)

    for g in range(KV_GROUPS):
        keep = lo_half if g == 0 else jnp.logical_not(lo_half)
        for p in range(HPG):
            qa_scr[p * TQ:(p + 1) * TQ, 0:KV_WIDTH] = jnp.where(
                keep, q_ref[:, KV_WIDTH * p:KV_WIDTH * (p + 1)], jnp.zeros((), BF16))
        qg = qa_scr[:, 0:KV_WIDTH]

        s_scr[:, 0:ncp] = _dot_nt(qg, kc_ref[0])
        for c in range(n_chunks):
            sl = slice(c * RC, (c + 1) * RC)
            head, r0 = divmod(c * RC, TQ)
            s = s_scr[sl, 0:ncp] + bc_ref[g, head, r0:r0 + RC, :]
            m = jnp.max(s, axis=1, keepdims=True)
            e = jnp.exp(s - m)
            l = jnp.sum(e, axis=1, keepdims=True)
            tq = t0 + r0 + lax.broadcasted_iota(jnp.int32, (RC, 1), 0)
            inv = jnp.where(tq >= CMP_BLOCK - 1, 1.0 / jnp.maximum(l, 1e-30), 0.0)
            p_c = e * inv
            s_scr[sl, 0:ncp] = p_c
            p_scr[sl, 0:ncp] = p_c.astype(BF16)
        out_scr[g] = gate_rows(g, 0) * _dot(p_scr[:, 0:ncp], vc_ref[0])

        psum = s_scr[0:TQ, 0:ncp]
        for p in range(1, HPG):
            psum = psum + s_scr[p * TQ:(p + 1) * TQ, 0:ncp]
        hi = psum.astype(BF16)
        r1 = psum - hi.astype(F32)
        mid = r1.astype(BF16)
        lo = (r1 - mid.astype(F32)).astype(BF16)
        ovt = ovt_ref[...]
        imp_t = _dot_nt(ovt, hi) + _dot_nt(ovt, mid) + _dot_nt(ovt, lo)

        score = jnp.where(valid, imp_t + bonus, NEG_INF)
        cnt = jnp.zeros((nbp, TQ), jnp.int32)
        for ii in range(nbp):
            si = score[ii:ii + 1, :]
            beats = (si > score) | ((si == score) & (blk > ii))
            cnt = cnt + beats.astype(jnp.int32)
        sel = valid & (cnt < SLC_TOPK)
        mneg_t = jnp.where(sel, 0.0, NEG_INF).astype(BF16)
        mneg_t = jnp.concatenate([mneg_t, jnp.zeros((KV_WIDTH - nbp, TQ), BF16)], axis=0)
        mneg = _dot_nt(eye_ref[...], mneg_t).astype(BF16)
        for p in range(HPG):
            qa_scr[p * TQ:(p + 1) * TQ, KV_WIDTH:2 * KV_WIDTH] = mneg

        def sel_key(j):
            return jnp.concatenate(
                [ktile(ks_ref, j), et_ref[pl.ds(pl.multiple_of(j * TQ, TQ), TQ), :]], axis=1)

        attend(g, qa_scr[...], sel_key(i), ktile(vs_ref, i), True, table_off=0)

        @pl.when(i > 0)
        def _():
            attend(g, qa_scr[...], sel_key(jprev), ktile(vs_ref, jprev), False, table_off=TQ)

        def far_body(j, carry):
            attend(g, qa_scr[...], sel_key(j), ktile(vs_ref, j), False)
            return carry

        lax.fori_loop(0, jnp.maximum(i - 1, 0), far_body, 0)
        finish(g, 1)

        attend(g, qa_scr[:, 0:KV_WIDTH], ktile(kw_ref, i), ktile(vw_ref, i), True, table_off=0)

        @pl.when(i > 0)
        def _():
            attend(g, qa_scr[:, 0:KV_WIDTH], ktile(kw_ref, jprev), ktile(vw_ref, jprev), False,
                   table_off=2 * TQ)

        finish(g, 2)

    for p in range(HPG):
        sl = slice(p * TQ, (p + 1) * TQ)
        o_ref[:, KV_WIDTH * p:KV_WIDTH * (p + 1)] = jnp.where(
            lo_half, out_scr[0, sl, :], out_scr[1, sl, :]).astype(BF16)


def _nsa(rb_far, main, gl, kcmp, vcmp, kvsw, bias_c, near, ovt, etab, eye, b, s):
    ncp = kcmp.shape[1]
    nbp = ovt.shape[0]
    nq = s // TQ
    rows = HPG * TQ
    kv_spec = lambda col: pl.BlockSpec((1, s, KV_WIDTH), lambda bi, i: (bi, 0, col))
    kvsw3 = kvsw.reshape(b, s, 4 * KV_WIDTH)
    return pl.pallas_call(
        functools.partial(_nsa_kernel, ncp=ncp, nbp=nbp),
        grid=(b, nq),
        in_specs=[pl.BlockSpec(memory_space=pltpu.SMEM),
                  pl.BlockSpec((TQ, NSA_WIDTH), lambda bi, i: (bi * nq + i, 0)),
                  pl.BlockSpec((TQ, GATE_LANES), lambda bi, i: (bi * nq + i, 0)),
                  pl.BlockSpec((1, ncp, KV_WIDTH), lambda bi, i: (bi, 0, 0)),
                  pl.BlockSpec((1, ncp, KV_WIDTH), lambda bi, i: (bi, 0, 0)),
                  kv_spec(0), kv_spec(1), kv_spec(2), kv_spec(3),
                  pl.BlockSpec((KV_GROUPS, HPG, TQ, ncp), lambda bi, i: (0, 0, i, 0)),
                  _const_spec((KV_GROUPS, rows, 3 * TQ)),
                  _const_spec((nbp, ncp)),
                  _const_spec((s, KV_WIDTH)),
                  _const_spec((TQ, TQ))],
        out_specs=pl.BlockSpec((TQ, NSA_WIDTH), lambda bi, i: (bi * nq + i, 0)),
        out_shape=jax.ShapeDtypeStruct((b * s, NSA_WIDTH), BF16),
        scratch_shapes=[pltpu.VMEM((rows, 2 * KV_WIDTH), BF16),
                        pltpu.VMEM((rows, TQ), F32),
                        pltpu.VMEM((rows, TQ), BF16),
                        pltpu.VMEM((rows, KV_WIDTH), F32),
                        pltpu.VMEM((rows, KV_WIDTH), F32),
                        pltpu.VMEM((rows, KV_WIDTH), F32),
                        pltpu.VMEM((KV_GROUPS, rows, KV_WIDTH), F32)],
        compiler_params=_params("parallel", "arbitrary"),
        name="nsa_attention",
    )(rb_far, main, gl, kcmp, vcmp, kvsw3, kvsw3, kvsw3, kvsw3, bias_c, near, ovt, etab, eye)


def _mixout_kernel(x_ref, on_ref, cb_ref, cc_ref, cx_ref, hc_ref, hx_ref, cw_ref, gnn_ref, gnc_ref,
                   wo_ref, g1_ref, out_ref, *, tiles_per_seq, tm):
    first = (pl.program_id(0) % tiles_per_seq) == 0
    u = cc_ref[...].astype(F32) * cx_ref[...].astype(F32)
    halo = hc_ref[...].astype(F32) * hx_ref[...].astype(F32)
    halo = jnp.where(first, 0.0, halo)
    ue = jnp.concatenate([halo, u], axis=0)
    u1 = pltpu.roll(ue, 1, 0)[8:]
    u2 = pltpu.roll(ue, 2, 0)[8:]
    y = cw_ref[0:1, :] * u + cw_ref[1:2, :] * u1 + cw_ref[2:3, :] * u2
    o_conv = cb_ref[...].astype(F32) * y
    merged = jnp.concatenate([_rms(on_ref[...].astype(F32), gnn_ref[...]),
                              _rms(o_conv, gnc_ref[...])], axis=-1).astype(BF16)
    h = _dot(merged, wo_ref[...])
    out_ref[...] = x_ref[...] + _rms(h, g1_ref[...])


def _mixout(x2, o_nsa, main, cw, gnn, gnc, wo, g1, s, tm):
    t, d = x2.shape
    tps = s // tm
    hb = tm // 8
    halo_spec = lambda col: pl.BlockSpec((8, CONV_WIDTH), lambda i: (jnp.maximum(i * hb - 1, 0), col))
    return pl.pallas_call(
        functools.partial(_mixout_kernel, tiles_per_seq=tps, tm=tm),
        grid=(t // tm,),
        in_specs=[pl.BlockSpec((tm, d), lambda i: (i, 0)),
                  pl.BlockSpec((tm, NSA_WIDTH), lambda i: (i, 0)),
                  pl.BlockSpec((tm, CONV_WIDTH), lambda i: (i, 1)),
                  pl.BlockSpec((tm, CONV_WIDTH), lambda i: (i, 2)),
                  pl.BlockSpec((tm, CONV_WIDTH), lambda i: (i, 3)),
                  halo_spec(2), halo_spec(3),
                  _const_spec((8, CONV_WIDTH)),
                  _const_spec((1, NSA_WIDTH)), _const_spec((1, CONV_WIDTH)),
                  _const_spec((NSA_WIDTH + CONV_WIDTH, d)),
                  _const_spec((1, d))],
        out_specs=pl.BlockSpec((tm, d), lambda i: (i, 0)),
        out_shape=jax.ShapeDtypeStruct((t, d), F32),
        compiler_params=_params("parallel"),
        name="mix_out",
    )(x2, o_nsa, main, main, main, main, main, cw, gnn, gnc, wo, g1)


def _memattn_kernel(x_ref, g2_ref, wq_ref, kv_ref, wo_ref, g3_ref, out_ref, *, tm):
    x = x_ref[...]
    q = _dot(_rms(x, g2_ref[...]).astype(BF16), wq_ref[...]).astype(BF16)
    lane = lax.broadcasted_iota(jnp.int32, (tm, 2 * MEM_HEAD_DIM), 1)
    lo_half = lane < MEM_HEAD_DIM
    zero = jnp.zeros((), BF16)
    slots = []
    for sl in range(MEM_HEADS // 2):
        qs = q[:, 128 * sl:128 * (sl + 1)]
        q2 = jnp.concatenate([jnp.where(lo_half, qs, zero), jnp.where(lo_half, zero, qs)], axis=0)
        k = kv_ref[0, :, 128 * sl:128 * (sl + 1)]
        v = kv_ref[0, :, MEM_WIDTH + 128 * sl:MEM_WIDTH + 128 * (sl + 1)]
        lg = _dot_nt(q2, k)
        e = jnp.exp(lg - jnp.max(lg, axis=-1, keepdims=True))
        p = e / jnp.sum(e, axis=-1, keepdims=True)
        o2 = _dot(p.astype(BF16), v)
        slots.append(jnp.where(lo_half, o2[:tm], o2[tm:]))
    o = jnp.concatenate(slots, axis=-1).astype(BF16)
    h = _dot(o, wo_ref[...])
    out_ref[...] = x + _rms(h, g3_ref[...])


def _memattn(x2, g2, wq, memkv, wo, g3, s, tm):
    t, d = x2.shape
    tps = s // tm
    ml = memkv.shape[1]
    return pl.pallas_call(
        functools.partial(_memattn_kernel, tm=tm),
        grid=(t // tm,),
        in_specs=[pl.BlockSpec((tm, d), lambda i: (i, 0)),
                  _const_spec((1, d)),
                  _const_spec((d, MEM_WIDTH)),
                  pl.BlockSpec((1, ml, 2 * MEM_WIDTH), lambda i: (i // tps, 0, 0)),
                  _const_spec((MEM_WIDTH, d)),
                  _const_spec((1, d))],
        out_specs=pl.BlockSpec((tm, d), lambda i: (i, 0)),
        out_shape=jax.ShapeDtypeStruct((t, d), F32),
        compiler_params=_params("parallel"),
        name="mem_attention",
    )(x2, g2, wq, memkv, wo, g3)


def _ffn_kernel(x_ref, g4_ref, wg_ref, wu_ref, wd_ref, g5_ref, out_ref, xn_ref, acc_ref):
    j = pl.program_id(1)

    @pl.when(j == 0)
    def _():
        xn_ref[...] = _rms(x_ref[...], g4_ref[...]).astype(BF16)

    xn = xn_ref[...]
    a = _dot(xn, wg_ref[...])
    u = _dot(xn, wu_ref[...])
    hid = (a * _sigmoid(a) * u).astype(BF16)
    part = _dot(hid, wd_ref[...])

    @pl.when(j == 0)
    def _():
        acc_ref[...] = part

    @pl.when(j > 0)
    def _():
        acc_ref[...] += part

    @pl.when(j == pl.num_programs(1) - 1)
    def _():
        out_ref[...] = x_ref[...] + _rms(acc_ref[...], g5_ref[...])


def _ffn(x2, g4, wg, wu, wd, g5, tm, th):
    t, d = x2.shape
    hdim = wg.shape[1]
    return pl.pallas_call(
        _ffn_kernel,
        grid=(t // tm, hdim // th),
        in_specs=[pl.BlockSpec((tm, d), lambda i, j: (i, 0)),
                  _const_spec((1, d)),
                  pl.BlockSpec((d, th), lambda i, j: (0, j)),
                  pl.BlockSpec((d, th), lambda i, j: (0, j)),
                  pl.BlockSpec((th, d), lambda i, j: (j, 0)),
                  _const_spec((1, d))],
        out_specs=pl.BlockSpec((tm, d), lambda i, j: (i, 0)),
        out_shape=jax.ShapeDtypeStruct((t, d), F32),
        scratch_shapes=[pltpu.VMEM((tm, d), BF16), pltpu.VMEM((tm, d), F32)],
        compiler_params=_params("parallel", "arbitrary"),
        name="swiglu",
    )(x2, g4, wg, wu, wd, g5)


def _t5_bucket(dist):
    n = jnp.maximum(dist, 0)
    max_exact = N_BUCKETS // 2
    nf = jnp.maximum(n, 1).astype(F32)
    large = max_exact + (jnp.log(nf / max_exact) / math.log(MAX_DISTANCE / max_exact)
                         * (N_BUCKETS - max_exact)).astype(jnp.int32)
    large = jnp.minimum(large, N_BUCKETS - 1)
    return jnp.where(n < max_exact, n, large)


def _bias_table(rel_bias, dist, visible):
    bucket = _t5_bucket(dist)[None]
    out = jnp.zeros((rel_bias.shape[0],) + dist.shape, F32)
    for k in range(N_BUCKETS):
        out = jnp.where(bucket == k, rel_bias[:, k].reshape((-1,) + (1,) * dist.ndim), out)
    return jnp.where(visible[None], out, NEG_INF)


def _pair_heads(a, axis):
    chunks = jnp.split(a, NSA_HEADS, axis=axis)
    order = []
    for p in range(HPG):
        order += [chunks[p], chunks[HPG + p]]
    return jnp.concatenate(order, axis=axis)


def _blockdiag_first(w1):
    hdim = w1.shape[1]
    w1r = w1.reshape(CMP_BLOCK, HEAD_DIM, hdim)
    eye = jnp.eye(KV_GROUPS, dtype=w1.dtype)
    halves = []
    for part in (w1r[:CMP_STRIDE], w1r[CMP_STRIDE:]):
        halves.append(jnp.einsum('jdm,gh->jgdhm', part, eye)
                      .reshape(CMP_STRIDE * KV_WIDTH, KV_GROUPS * hdim).astype(BF16))
    return halves


def _blockdiag_second(w2):
    eye = jnp.eye(KV_GROUPS, dtype=w2.dtype)
    return jnp.einsum('md,gh->gmhd', w2, eye).reshape(KV_GROUPS * w2.shape[0], KV_WIDTH).astype(BF16)


def kernel(x, mem, rel_bias, norms, mem_norm, gn_nsa, gn_conv, w_in, w_out, cmp_pe, cmp_k_w1, cmp_k_w2,
           cmp_v_w1, cmp_v_w2, conv_w, mem_wq, mem_wkv, mem_wo, ffn_wg, ffn_wu, ffn_wd):
    b, s, d = x.shape
    depth = w_in.shape[0]
    ml = mem.shape[1]
    t = b * s
    assert s % TQ == 0 and s // SLC_BLOCK <= 32 and TQ == WINDOW
    nch = s // CMP_STRIDE
    ncp = -(-nch // 128) * 128
    assert ncp <= TQ
    n_cmp = nch - 1
    n_slc = s // SLC_BLOCK
    nbp = 32
    tm = 512

    r = jnp.arange(TQ)[:, None]
    c = jnp.arange(TQ)[None, :]
    near = jnp.concatenate([_bias_table(rel_bias, r - c, r >= c),
                            _bias_table(rel_bias, r + TQ - c, r + TQ - c > 0),
                            _bias_table(rel_bias, r + TQ - c, r < c)],
                           axis=-1).reshape(KV_GROUPS, HPG * TQ, 3 * TQ)
    dist_c = jnp.arange(s)[:, None] - CMP_STRIDE * jnp.arange(ncp)[None, :] - (CMP_BLOCK - 1)
    bias_c = _bias_table(rel_bias, dist_c, dist_c >= 0).reshape(KV_GROUPS, HPG, s, ncp)
    rb_far = rel_bias[:, N_BUCKETS - 1]
    cs = np.arange(ncp) * CMP_STRIDE
    ss = np.arange(nbp) * SLC_BLOCK
    ov = np.clip(np.minimum(cs[:, None] + CMP_BLOCK, ss[None, :] + SLC_BLOCK)
                 - np.maximum(cs[:, None], ss[None, :]), 0, None) / CMP_BLOCK
    ov[n_cmp:, :] = 0.0
    ov[:, n_slc:] = 0.0
    ovt = jnp.asarray(ov.T, dtype=BF16)
    etab = jnp.asarray((np.arange(s)[:, None] // SLC_BLOCK) == np.arange(KV_WIDTH)[None, :], dtype=BF16)
    eye = jnp.eye(TQ, dtype=BF16)

    x2 = x.reshape(t, d)
    mem2 = mem.reshape(b * ml, d)
    for l in range(depth):
        wq, wkc, wvc, wks, wvs, wkw, wvw, wgl, wcb, wcc, wcx = jnp.split(
            w_in[l], np.cumsum((NSA_WIDTH,) + (KV_WIDTH,) * 6 + (3 * NSA_HEADS,) + (CONV_WIDTH,) * 2), axis=1)
        wgl = jnp.pad(wgl, ((0, 0), (0, GATE_LANES - wgl.shape[1])))
        w_all = jnp.concatenate([_pair_heads(wq, 1) * (HEAD_DIM ** -0.5), wcb, wcc, wcx,
                                 wkc, wvc, wks, wvs, wkw, wvw, wgl], axis=1).astype(BF16)
        gnn = _pair_heads(gn_nsa[l], 0)[None, :]
        wo = jnp.concatenate([_pair_heads(w_out[l][:NSA_WIDTH], 0), w_out[l][NSA_WIDTH:]], axis=0).astype(BF16)
        pe = cmp_pe[l].reshape(2, CMP_STRIDE, 1, HEAD_DIM)
        pe2 = jnp.broadcast_to(pe, (2, CMP_STRIDE, KV_GROUPS, HEAD_DIM)).reshape(2, CMP_STRIDE * KV_WIDTH)
        wka, wkb = _blockdiag_first(cmp_k_w1[l])
        wva, wvb = _blockdiag_first(cmp_v_w1[l])
        cw = jnp.pad(conv_w[l], ((0, 8 - CONV_K), (0, 0)))

        main, kc, vc, kvsw, gl = _inproj(x2, norms[l, 0][None, :], w_all, tm)
        kcmp, vcmp = _compress(kc.reshape(b, nch, CMP_STRIDE * KV_WIDTH),
                               vc.reshape(b, nch, CMP_STRIDE * KV_WIDTH), pe2,
                               wka, wkb, _blockdiag_second(cmp_k_w2[l]),
                               wva, wvb, _blockdiag_second(cmp_v_w2[l]), ncp)
        o_nsa = _nsa(rb_far, main, gl, kcmp, vcmp, kvsw, bias_c, near, ovt, etab, eye, b, s)
        x2 = _mixout(x2, o_nsa, main, cw, gnn, gn_conv[l][None, :], wo, norms[l, 1][None, :], s, tm)

        memkv = _norm_matmul(mem2, mem_norm[l][None, :], mem_wkv[l].astype(BF16), ml, BF16, "mem_kv")
        x2 = _memattn(x2, norms[l, 2][None, :], (mem_wq[l] * (MEM_HEAD_DIM ** -0.5)).astype(BF16),
                      memkv.reshape(b, ml, 2 * MEM_WIDTH), mem_wo[l].astype(BF16), norms[l, 3][None, :], s, tm)

        x2 = _ffn(x2, norms[l, 4][None, :], ffn_wg[l].astype(BF16), ffn_wu[l].astype(BF16),
                  ffn_wd[l].astype(BF16), norms[l, 5][None, :], 1024 if t % 1024 == 0 else tm, 256)
    return x2.reshape(b, s, d)
```

```python
import functools
import math

import numpy as np
import jax
import jax.numpy as jnp
from jax import lax
from jax.experimental import pallas as pl
from jax.experimental.pallas import tpu as pltpu

F32 = jnp.float32
BF16 = jnp.bfloat16

HEAD_DIM = 64
NSA_HEADS = 8
KV_GROUPS = 2
HPG = NSA_HEADS // KV_GROUPS
NSA_WIDTH = NSA_HEADS * HEAD_DIM
KV_WIDTH = KV_GROUPS * HEAD_DIM
CONV_WIDTH = 512
CONV_K = 3
CMP_STRIDE = 16
CMP_BLOCK = 32
CMP_HIDDEN = 128
SLC_BLOCK = 64
SLC_TOPK = 8
WINDOW = 256
FORCE_BONUS = 1.0e4
N_BUCKETS = 32
MAX_DISTANCE = 128
MEM_HEADS = 4
MEM_HEAD_DIM = 64
MEM_WIDTH = MEM_HEADS * MEM_HEAD_DIM
RMS_EPS = 1e-6
NEG_INF = -1e30
GATE_LANES = 128

TQ = 256
RC = 64
NT_DIMS = (((1,), (1,)), ((), ()))
LOG2E = math.log2(math.e)

VMEM_LIMIT = 56 * 1024 * 1024


def _rms(x, g):
    return x * lax.rsqrt(jnp.mean(x * x, axis=-1, keepdims=True) + RMS_EPS) * g


def _dot(a, b):
    return jnp.dot(a, b, preferred_element_type=F32)


def _dot_nt(a, b):
    return lax.dot_general(a, b, NT_DIMS, preferred_element_type=F32)


def _sigmoid(x):
    return 1.0 / (1.0 + jnp.exp(-x))


def _params(*sem):
    return pltpu.CompilerParams(dimension_semantics=sem, vmem_limit_bytes=VMEM_LIMIT)


def _const_spec(shape, single=False):
    index_map = lambda *_: (0,) * len(shape)
    if single:
        return pl.BlockSpec(shape, index_map, pipeline_mode=pl.Buffered(1))
    return pl.BlockSpec(shape, index_map)


def _inproj_kernel(x_ref, g_ref, w_ref, main_ref, kc_ref, vc_ref, kvsw_ref, gl_ref):
    xn = _rms(x_ref[...], g_ref[...]).astype(BF16)

    def mm(lo, hi):
        return _dot(xn, w_ref[:, lo:hi])

    for c in range(4):
        main_ref[:, 512 * c:512 * (c + 1)] = mm(512 * c, 512 * (c + 1)).astype(BF16)
    r = mm(2048, 2304)
    kc_ref[...] = r[:, :KV_WIDTH].astype(BF16)
    vc_ref[...] = r[:, KV_WIDTH:].astype(BF16)
    kvsw_ref[...] = mm(2304, 2816).astype(BF16)
    gl_ref[...] = mm(2816, 2944)


def _inproj(x2, g, w, tm):
    t, d = x2.shape
    nw = w.shape[1]
    return pl.pallas_call(
        _inproj_kernel,
        grid=(t // tm,),
        in_specs=[pl.BlockSpec((tm, d), lambda i: (i, 0)),
                  _const_spec((1, d)),
                  _const_spec((d, nw))],
        out_specs=[pl.BlockSpec((tm, 2048), lambda i: (i, 0)),
                   pl.BlockSpec((tm, KV_WIDTH), lambda i: (i, 0)),
                   pl.BlockSpec((tm, KV_WIDTH), lambda i: (i, 0)),
                   pl.BlockSpec((tm, 4 * KV_WIDTH), lambda i: (i, 0)),
                   pl.BlockSpec((tm, GATE_LANES), lambda i: (i, 0))],
        out_shape=[jax.ShapeDtypeStruct((t, 2048), BF16),
                   jax.ShapeDtypeStruct((t, KV_WIDTH), BF16),
                   jax.ShapeDtypeStruct((t, KV_WIDTH), BF16),
                   jax.ShapeDtypeStruct((t, 4 * KV_WIDTH), BF16),
                   jax.ShapeDtypeStruct((t, GATE_LANES), F32)],
        compiler_params=_params("parallel"),
        name="inproj",
    )(x2, g, w)


def _norm_matmul_kernel(x_ref, g_ref, w_ref, o_ref):
    xn = _rms(x_ref[...], g_ref[...]).astype(BF16)
    o_ref[...] = _dot(xn, w_ref[...]).astype(o_ref.dtype)


def _norm_matmul(x2, g, w, tm, out_dtype, name):
    t, d = x2.shape
    n = w.shape[1]
    return pl.pallas_call(
        _norm_matmul_kernel,
        grid=(t // tm,),
        in_specs=[pl.BlockSpec((tm, d), lambda i: (i, 0)),
                  _const_spec((1, d)),
                  _const_spec((d, n))],
        out_specs=pl.BlockSpec((tm, n), lambda i: (i, 0)),
        out_shape=jax.ShapeDtypeStruct((t, n), out_dtype),
        compiler_params=_params("parallel"),
        name=name,
    )(x2, g, w)


def _gelu_tanh(x):
    c = math.sqrt(2.0 / math.pi)
    return x * (0.5 * (1.0 + jnp.tanh(c * (x + 0.044715 * (x * x * x)))))


def _compress_kernel(kc_ref, vc_ref, pe_ref, wka_ref, wkb_ref, wk2_ref, wva_ref, wvb_ref, wv2_ref,
                     ko_ref, vo_ref, *, nch, ncp):
    row = lax.broadcasted_iota(jnp.int32, (nch, KV_WIDTH), 0)

    def one(x_ref, wa_ref, wb_ref, w2_ref, o_ref):
        x = x_ref[0].astype(F32)
        a = _dot((x + pe_ref[0:1, :]).astype(BF16), wa_ref[...])
        b = _dot((x + pe_ref[1:2, :]).astype(BF16), wb_ref[...])
        hid = _gelu_tanh(a + pltpu.roll(b, nch - 1, 0))
        out = _dot(hid.astype(BF16), w2_ref[...])
        out = jnp.where(row < nch - 1, out, 0.0).astype(BF16)
        if ncp > nch:
            out = jnp.concatenate([out, jnp.zeros((ncp - nch, KV_WIDTH), BF16)], axis=0)
        o_ref[0] = out

    one(kc_ref, wka_ref, wkb_ref, wk2_ref, ko_ref)
    one(vc_ref, wva_ref, wvb_ref, wv2_ref, vo_ref)


def _compress(kc3, vc3, pe2, wka, wkb, wk2, wva, wvb, wv2, ncp):
    b, nch, cw = kc3.shape
    hid2 = wka.shape[1]
    xspec = pl.BlockSpec((1, nch, cw), lambda i: (i, 0, 0))
    ospec = pl.BlockSpec((1, ncp, KV_WIDTH), lambda i: (i, 0, 0))
    return pl.pallas_call(
        functools.partial(_compress_kernel, nch=nch, ncp=ncp),
        grid=(b,),
        in_specs=[xspec, xspec, _const_spec((2, cw)),
                  _const_spec((cw, hid2)), _const_spec((cw, hid2)), _const_spec((hid2, KV_WIDTH)),
                  _const_spec((cw, hid2)), _const_spec((cw, hid2)), _const_spec((hid2, KV_WIDTH))],
        out_specs=[ospec, ospec],
        out_shape=[jax.ShapeDtypeStruct((b, ncp, KV_WIDTH), BF16)] * 2,
        compiler_params=_params("parallel"),
        name="compress",
    )(kc3, vc3, pe2, wka, wkb, wk2, wva, wvb, wv2)


def _nsa_kernel(far_ref, q_ref, gl_ref, kc_ref, vc_ref, ks_ref, vs_ref, kw_ref, vw_ref,
                bc_ref, nt_ref, ovt_ref, et_ref, eye_ref, o_ref,
                qa_scr, s_all, p_scr, mx_scr, m_scr, acc_scr, out_scr, *, ncp, nbp, seq):
    i = pl.program_id(1)
    t0 = i * TQ
    grows = HPG * TQ
    rows = KV_GROUPS * grows
    jprev = jnp.maximum(i - 1, 0)
    n_far = jnp.maximum(i - 1, 0)
    n_chunks = rows // RC

    lane = lax.broadcasted_iota(jnp.int32, (TQ, KV_WIDTH), 1)
    lo_half = lane < HEAD_DIM

    blk = lax.broadcasted_iota(jnp.int32, (nbp, TQ), 0)
    cur = (t0 + lax.broadcasted_iota(jnp.int32, (nbp, TQ), 1)) >> 6
    valid = blk <= cur
    forced = (blk == 0) | (blk == cur) | (blk == cur - 1)
    bonus = jnp.where(forced, FORCE_BONUS, 0.0).astype(F32)

    gates = _sigmoid(gl_ref[...])

    def ktile(ref, j):
        return ref[0, pl.ds(pl.multiple_of(j * TQ, TQ), TQ), :]

    def key_aug(ref, j, block_mask):
        extra = et_ref[pl.ds(pl.multiple_of(j * TQ, TQ), TQ), :] if block_mask else et_ref[seq:seq + TQ, :]
        return jnp.concatenate([ktile(ref, j), extra], axis=1)

    def logits(slot, k_aug, table_off, first):
        sc = _dot_nt(qa_scr[...], k_aug)
        if table_off is not None:
            sc = sc + nt_ref[:, table_off:table_off + TQ]
        s_all[slot] = sc
        part = jnp.maximum(sc[:, 0:KV_WIDTH], sc[:, KV_WIDTH:])
        mx_scr[...] = part if first else jnp.maximum(mx_scr[...], part)

    def set_max():
        m_scr[...] = jnp.broadcast_to(jnp.max(mx_scr[...], axis=1, keepdims=True), (rows, KV_WIDTH))

    ones_cols = jnp.ones((TQ, KV_WIDTH), BF16)

    def accumulate(slot, v_tile, first):
        for c in range(n_chunks):
            sl = slice(c * RC, (c + 1) * RC)
            m = m_scr[sl, :]
            e = jnp.exp2(s_all[slot, sl, :] - jnp.concatenate([m, m], axis=1))
            p_scr[sl, :] = e.astype(BF16)
        pv = _dot(p_scr[...], jnp.concatenate([v_tile, ones_cols], axis=1))
        acc_scr[...] = pv if first else acc_scr[...] + pv

    def gate_rows(k):
        return jnp.concatenate(
            [jnp.broadcast_to(gates[:, 3 * h + k:3 * h + k + 1], (TQ, KV_WIDTH)) for h in range(NSA_HEADS)],
            axis=0)

    def finish(k):
        out_scr[...] = out_scr[...] + gate_rows(k) * (acc_scr[:, 0:KV_WIDTH] / acc_scr[:, KV_WIDTH:])

    for h in range(NSA_HEADS):
        g, p = divmod(h, HPG)
        keep = lo_half if g == 0 else jnp.logical_not(lo_half)
        qa_scr[h * TQ:(h + 1) * TQ, 0:KV_WIDTH] = jnp.where(
            keep, q_ref[:, KV_WIDTH * p:KV_WIDTH * (p + 1)], jnp.zeros((), BF16))

    s_all[0, :, 0:ncp] = _dot_nt(qa_scr[:, 0:KV_WIDTH], kc_ref[0])
    for c in range(n_chunks):
        sl = slice(c * RC, (c + 1) * RC)
        h, r0 = divmod(c * RC, TQ)
        s = s_all[0, sl, 0:ncp] + bc_ref[h, r0:r0 + RC, :]
        m = jnp.max(s, axis=1, keepdims=True)
        e = jnp.exp2(s - m)
        l = jnp.sum(e, axis=1, keepdims=True)
        tq = t0 + r0 + lax.broadcasted_iota(jnp.int32, (RC, 1), 0)
        inv = jnp.where(tq >= CMP_BLOCK - 1, 1.0 / jnp.maximum(l, 1e-30), 0.0)
        p_c = e * inv
        s_all[0, sl, 0:ncp] = p_c
        p_scr[sl, 0:ncp] = p_c.astype(BF16)
    out_scr[...] = gate_rows(0) * _dot(p_scr[:, 0:ncp], vc_ref[0])

    for g in range(KV_GROUPS):
        psum = s_all[0, g * grows:g * grows + TQ, 0:ncp]
        for p in range(1, HPG):
            psum = psum + s_all[0, g * grows + p * TQ:g * grows + (p + 1) * TQ, 0:ncp]
        hi = psum.astype(BF16)
        r1 = psum - hi.astype(F32)
        mid = r1.astype(BF16)
        lo = (r1 - mid.astype(F32)).astype(BF16)
        ovt = ovt_ref[...]
        imp_t = _dot_nt(ovt, hi) + _dot_nt(ovt, mid) + _dot_nt(ovt, lo)

        score = jnp.where(valid, imp_t + bonus, NEG_INF)
        cnt = jnp.zeros((nbp, TQ), jnp.int32)
        for ii in range(nbp):
            si = score[ii:ii + 1, :]
            beats = (si > score) | ((si == score) & (blk > ii))
            cnt = cnt + beats.astype(jnp.int32)
        sel = valid & (cnt < SLC_TOPK)
        mneg_t = jnp.where(sel, 0.0, NEG_INF).astype(BF16)
        mneg_t = jnp.concatenate([mneg_t, jnp.zeros((KV_WIDTH - nbp, TQ), BF16)], axis=0)
        mneg = _dot_nt(eye_ref[...], mneg_t)
        for p in range(HPG):
            h = g * HPG + p
            extra = jnp.where(lane == nbp, far_ref[0, h], jnp.where(lane == nbp + 1, far_ref[1, h], mneg))
            qa_scr[h * TQ:(h + 1) * TQ, KV_WIDTH:2 * KV_WIDTH] = extra.astype(BF16)

    logits(i, key_aug(ks_ref, i, True), 0, True)

    @pl.when(i > 0)
    def _():
        logits(jprev, key_aug(ks_ref, jprev, True), TQ, False)

    def far_logits(j, carry):
        logits(j, key_aug(ks_ref, j, True), None, False)
        return carry

    lax.fori_loop(0, n_far, far_logits, 0)
    set_max()
    accumulate(i, ktile(vs_ref, i), True)

    @pl.when(i > 0)
    def _():
        accumulate(jprev, ktile(vs_ref, jprev), False)

    def far_accumulate(j, carry):
        accumulate(j, ktile(vs_ref, j), False)
        return carry

    lax.fori_loop(0, n_far, far_accumulate, 0)
    finish(1)

    logits(0, key_aug(kw_ref, i, False), 0, True)

    @pl.when(i > 0)
    def _():
        logits(1, key_aug(kw_ref, jprev, False), 2 * TQ, False)

    set_max()
    accumulate(0, ktile(vw_ref, i), True)

    @pl.when(i > 0)
    def _():
        accumulate(1, ktile(vw_ref, jprev), False)

    finish(2)

    for p in range(HPG):
        o_ref[:, KV_WIDTH * p:KV_WIDTH * (p + 1)] = jnp.where(
            lo_half, out_scr[p * TQ:(p + 1) * TQ, :], out_scr[grows + p * TQ:grows + (p + 1) * TQ, :]
        ).astype(BF16)


def _nsa(far_hl, main, gl, kcmp, vcmp, kvsw, bias_c, near, ovt, etab, eye, b, s):
    ncp = kcmp.shape[1]
    nbp = ovt.shape[0]
    nq = s // TQ
    rows = NSA_HEADS * TQ
    kv_spec = lambda col: pl.BlockSpec((1, s, KV_WIDTH), lambda bi, i: (bi, 0, col))
    kvsw3 = kvsw.reshape(b, s, 4 * KV_WIDTH)
    return pl.pallas_call(
        functools.partial(_nsa_kernel, ncp=ncp, nbp=nbp, seq=s),
        grid=(b, nq),
        in_specs=[pl.BlockSpec(memory_space=pltpu.SMEM),
                  pl.BlockSpec((TQ, NSA_WIDTH), lambda bi, i: (bi * nq + i, 0)),
                  pl.BlockSpec((TQ, GATE_LANES), lambda bi, i: (bi * nq + i, 0)),
                  pl.BlockSpec((1, ncp, KV_WIDTH), lambda bi, i: (bi, 0, 0)),
                  pl.BlockSpec((1, ncp, KV_WIDTH), lambda bi, i: (bi, 0, 0)),
                  kv_spec(0), kv_spec(1), kv_spec(2), kv_spec(3),
                  pl.BlockSpec((NSA_HEADS, TQ, ncp), lambda bi, i: (0, i, 0)),
                  _const_spec((rows, 3 * TQ), single=True),
                  _const_spec((nbp, ncp), single=True),
                  _const_spec((s + TQ, KV_WIDTH), single=True),
                  _const_spec((TQ, TQ), single=True)],
        out_specs=pl.BlockSpec((TQ, NSA_WIDTH), lambda bi, i: (bi * nq + i, 0)),
        out_shape=jax.ShapeDtypeStruct((b * s, NSA_WIDTH), BF16),
        scratch_shapes=[pltpu.VMEM((rows, 2 * KV_WIDTH), BF16),
                        pltpu.VMEM((nq, rows, TQ), F32),
                        pltpu.VMEM((rows, TQ), BF16),
                        pltpu.VMEM((rows, KV_WIDTH), F32),
                        pltpu.VMEM((rows, KV_WIDTH), F32),
                        pltpu.VMEM((rows, 2 * KV_WIDTH), F32),
                        pltpu.VMEM((rows, KV_WIDTH), F32)],
        compiler_params=_params("parallel", "arbitrary"),
        name="nsa_attention",
    )(far_hl, main, gl, kcmp, vcmp, kvsw3, kvsw3, kvsw3, kvsw3, bias_c, near, ovt, etab, eye)


def _mixout_kernel(x_ref, on_ref, cb_ref, cc_ref, cx_ref, hc_ref, hx_ref, cw_ref, gnn_ref, gnc_ref,
                   wo_ref, g1_ref, out_ref, *, tiles_per_seq, tm):
    first = (pl.program_id(0) % tiles_per_seq) == 0
    u = cc_ref[...].astype(F32) * cx_ref[...].astype(F32)
    halo = hc_ref[...].astype(F32) * hx_ref[...].astype(F32)
    halo = jnp.where(first, 0.0, halo)
    ue = jnp.concatenate([halo, u], axis=0)
    u1 = pltpu.roll(ue, 1, 0)[8:]
    u2 = pltpu.roll(ue, 2, 0)[8:]
    y = cw_ref[0:1, :] * u + cw_ref[1:2, :] * u1 + cw_ref[2:3, :] * u2
    o_conv = cb_ref[...].astype(F32) * y
    merged = jnp.concatenate([_rms(on_ref[...].astype(F32), gnn_ref[...]),
                              _rms(o_conv, gnc_ref[...])], axis=-1).astype(BF16)
    h = _dot(merged, wo_ref[...])
    out_ref[...] = x_ref[...] + _rms(h, g1_ref[...])


def _mixout(x2, o_nsa, main, cw, gnn, gnc, wo, g1, s, tm):
    t, d = x2.shape
    tps = s // tm
    hb = tm // 8
    halo_spec = lambda col: pl.BlockSpec((8, CONV_WIDTH), lambda i: (jnp.maximum(i * hb - 1, 0), col))
    return pl.pallas_call(
        functools.partial(_mixout_kernel, tiles_per_seq=tps, tm=tm),
        grid=(t // tm,),
        in_specs=[pl.BlockSpec((tm, d), lambda i: (i, 0)),
                  pl.BlockSpec((tm, NSA_WIDTH), lambda i: (i, 0)),
                  pl.BlockSpec((tm, CONV_WIDTH), lambda i: (i, 1)),
                  pl.BlockSpec((tm, CONV_WIDTH), lambda i: (i, 2)),
                  pl.BlockSpec((tm, CONV_WIDTH), lambda i: (i, 3)),
                  halo_spec(2), halo_spec(3),
                  _const_spec((8, CONV_WIDTH)),
                  _const_spec((1, NSA_WIDTH)), _const_spec((1, CONV_WIDTH)),
                  _const_spec((NSA_WIDTH + CONV_WIDTH, d)),
                  _const_spec((1, d))],
        out_specs=pl.BlockSpec((tm, d), lambda i: (i, 0)),
        out_shape=jax.ShapeDtypeStruct((t, d), F32),
        compiler_params=_params("parallel"),
        name="mix_out",
    )(x2, o_nsa, main, main, main, main, main, cw, gnn, gnc, wo, g1)


def _memattn_kernel(x_ref, g2_ref, wq_ref, kv_ref, wo_ref, g3_ref, out_ref, *, tm):
    x = x_ref[...]
    q = _dot(_rms(x, g2_ref[...]).astype(BF16), wq_ref[...]).astype(BF16)
    lane = lax.broadcasted_iota(jnp.int32, (tm, 2 * MEM_HEAD_DIM), 1)
    lo_half = lane < MEM_HEAD_DIM
    zero = jnp.zeros((), BF16)
    slots = []
    for sl in range(MEM_HEADS // 2):
        qs = q[:, 128 * sl:128 * (sl + 1)]
        q2 = jnp.concatenate([jnp.where(lo_half, qs, zero), jnp.where(lo_half, zero, qs)], axis=0)
        k = kv_ref[0, :, 128 * sl:128 * (sl + 1)]
        v = kv_ref[0, :, MEM_WIDTH + 128 * sl:MEM_WIDTH + 128 * (sl + 1)]
        lg = _dot_nt(q2, k)
        e = jnp.exp(lg - jnp.max(lg, axis=-1, keepdims=True))
        p = e / jnp.sum(e, axis=-1, keepdims=True)
        o2 = _dot(p.astype(BF16), v)
        slots.append(jnp.where(lo_half, o2[:tm], o2[tm:]))
    o = jnp.concatenate(slots, axis=-1).astype(BF16)
    h = _dot(o, wo_ref[...])
    out_ref[...] = x + _rms(h, g3_ref[...])


def _memattn(x2, g2, wq, memkv, wo, g3, s, tm):
    t, d = x2.shape
    tps = s // tm
    ml = memkv.shape[1]
    return pl.pallas_call(
        functools.partial(_memattn_kernel, tm=tm),
        grid=(t // tm,),
        in_specs=[pl.BlockSpec((tm, d), lambda i: (i, 0)),
                  _const_spec((1, d)),
                  _const_spec((d, MEM_WIDTH)),
                  pl.BlockSpec((1, ml, 2 * MEM_WIDTH), lambda i: (i // tps, 0, 0)),
                  _const_spec((MEM_WIDTH, d)),
                  _const_spec((1, d))],
        out_specs=pl.BlockSpec((tm, d), lambda i: (i, 0)),
        out_shape=jax.ShapeDtypeStruct((t, d), F32),
        compiler_params=_params("parallel"),
        name="mem_attention",
    )(x2, g2, wq, memkv, wo, g3)


def _ffn_kernel(x_ref, g4_ref, wg_ref, wu_ref, wd_ref, g5_ref, out_ref, hid_ref, *, th):
    xn = _rms(x_ref[...], g4_ref[...]).astype(BF16)
    for c in range(hid_ref.shape[1] // th):
        cols = slice(c * th, (c + 1) * th)
        a = _dot(xn, wg_ref[:, cols])
        u = _dot(xn, wu_ref[:, cols])
        hid_ref[:, cols] = (a * _sigmoid(a) * u).astype(BF16)
    h = _dot(hid_ref[...], wd_ref[...])
    out_ref[...] = x_ref[...] + _rms(h, g5_ref[...])


def _ffn(x2, g4, wg, wu, wd, g5, tm, th):
    t, d = x2.shape
    hdim = wg.shape[1]
    return pl.pallas_call(
        functools.partial(_ffn_kernel, th=th),
        grid=(t // tm,),
        in_specs=[pl.BlockSpec((tm, d), lambda i: (i, 0)),
                  _const_spec((1, d)),
                  _const_spec((d, hdim), single=True),
                  _const_spec((d, hdim), single=True),
                  _const_spec((hdim, d), single=True),
                  _const_spec((1, d))],
        out_specs=pl.BlockSpec((tm, d), lambda i: (i, 0)),
        out_shape=jax.ShapeDtypeStruct((t, d), F32),
        scratch_shapes=[pltpu.VMEM((tm, hdim), BF16)],
        compiler_params=_params("parallel"),
        name="swiglu",
    )(x2, g4, wg, wu, wd, g5)


def _t5_bucket(dist):
    n = jnp.maximum(dist, 0)
    max_exact = N_BUCKETS // 2
    nf = jnp.maximum(n, 1).astype(F32)
    large = max_exact + (jnp.log(nf / max_exact) / math.log(MAX_DISTANCE / max_exact)
                         * (N_BUCKETS - max_exact)).astype(jnp.int32)
    large = jnp.minimum(large, N_BUCKETS - 1)
    return jnp.where(n < max_exact, n, large)


def _bias_table(rel_bias, dist, visible, minus=None):
    bucket = _t5_bucket(dist)[None]
    out = jnp.zeros((rel_bias.shape[0],) + dist.shape, F32)
    for k in range(N_BUCKETS):
        out = jnp.where(bucket == k, rel_bias[:, k].reshape((-1,) + (1,) * dist.ndim), out)
    if minus is not None:
        out = out - minus.reshape((-1,) + (1,) * dist.ndim)
    return jnp.where(visible[None], out * LOG2E, NEG_INF)


def _pair_heads(a, axis):
    chunks = jnp.split(a, NSA_HEADS, axis=axis)
    order = []
    for p in range(HPG):
        order += [chunks[p], chunks[HPG + p]]
    return jnp.concatenate(order, axis=axis)


def _blockdiag_first(w1):
    hdim = w1.shape[1]
    w1r = w1.reshape(CMP_BLOCK, HEAD_DIM, hdim)
    eye = jnp.eye(KV_GROUPS, dtype=w1.dtype)
    halves = []
    for part in (w1r[:CMP_STRIDE], w1r[CMP_STRIDE:]):
        halves.append(jnp.einsum('jdm,gh->jgdhm', part, eye)
                      .reshape(CMP_STRIDE * KV_WIDTH, KV_GROUPS * hdim).astype(BF16))
    return halves


def _blockdiag_second(w2):
    eye = jnp.eye(KV_GROUPS, dtype=w2.dtype)
    return jnp.einsum('md,gh->gmhd', w2, eye).reshape(KV_GROUPS * w2.shape[0], KV_WIDTH).astype(BF16)


def kernel(x, mem, rel_bias, norms, mem_norm, gn_nsa, gn_conv, w_in, w_out, cmp_pe, cmp_k_w1, cmp_k_w2,
           cmp_v_w1, cmp_v_w2, conv_w, mem_wq, mem_wkv, mem_wo, ffn_wg, ffn_wu, ffn_wd):
    b, s, d = x.shape
    depth = w_in.shape[0]
    ml = mem.shape[1]
    t = b * s
    assert s % TQ == 0 and s // SLC_BLOCK <= 32 and TQ == WINDOW
    nch = s // CMP_STRIDE
    ncp = -(-nch // 128) * 128
    assert ncp <= TQ
    n_cmp = nch - 1
    n_slc = s // SLC_BLOCK
    nbp = 32
    tm = 512

    r = jnp.arange(TQ)[:, None]
    c = jnp.arange(TQ)[None, :]
    far = rel_bias[:, N_BUCKETS - 1]
    near = jnp.concatenate([_bias_table(rel_bias, r - c, r >= c, far),
                            _bias_table(rel_bias, r + TQ - c, r + TQ - c > 0, far),
                            _bias_table(rel_bias, r + TQ - c, r < c, far)],
                           axis=-1).reshape(NSA_HEADS * TQ, 3 * TQ)
    dist_c = jnp.arange(s)[:, None] - CMP_STRIDE * jnp.arange(ncp)[None, :] - (CMP_BLOCK - 1)
    bias_c = _bias_table(rel_bias, dist_c, dist_c >= 0)
    far_hi = (far * LOG2E).astype(BF16).astype(F32)
    far_lo = (far * LOG2E - far_hi).astype(BF16).astype(F32)
    far_hl = jnp.stack([far_hi, far_lo])
    cs = np.arange(ncp) * CMP_STRIDE
    ss = np.arange(nbp) * SLC_BLOCK
    ov = np.clip(np.minimum(cs[:, None] + CMP_BLOCK, ss[None, :] + SLC_BLOCK)
                 - np.maximum(cs[:, None], ss[None, :]), 0, None) / CMP_BLOCK
    ov[n_cmp:, :] = 0.0
    ov[:, n_slc:] = 0.0
    ovt = jnp.asarray(ov.T, dtype=BF16)
    et = np.zeros((s + TQ, KV_WIDTH), np.float32)
    et[np.arange(s), np.arange(s) // SLC_BLOCK] = 1.0
    et[:, nbp:nbp + 2] = 1.0
    etab = jnp.asarray(et, dtype=BF16)
    eye = jnp.eye(TQ, dtype=BF16)

    x2 = x.reshape(t, d)
    mem2 = mem.reshape(b * ml, d)
    for l in range(depth):
        wq, wkc, wvc, wks, wvs, wkw, wvw, wgl, wcb, wcc, wcx = jnp.split(
            w_in[l], np.cumsum((NSA_WIDTH,) + (KV_WIDTH,) * 6 + (3 * NSA_HEADS,) + (CONV_WIDTH,) * 2), axis=1)
        wgl = jnp.pad(wgl, ((0, 0), (0, GATE_LANES - wgl.shape[1])))
        w_all = jnp.concatenate([_pair_heads(wq, 1) * (HEAD_DIM ** -0.5 * LOG2E), wcb, wcc, wcx,
                                 wkc, wvc, wks, wvs, wkw, wvw, wgl], axis=1).astype(BF16)
        gnn = _pair_heads(gn_nsa[l], 0)[None, :]
        wo = jnp.concatenate([_pair_heads(w_out[l][:NSA_WIDTH], 0), w_out[l][NSA_WIDTH:]], axis=0).astype(BF16)
        pe = cmp_pe[l].reshape(2, CMP_STRIDE, 1, HEAD_DIM)
        pe2 = jnp.broadcast_to(pe, (2, CMP_STRIDE, KV_GROUPS, HEAD_DIM)).reshape(2, CMP_STRIDE * KV_WIDTH)
        wka, wkb = _blockdiag_first(cmp_k_w1[l])
        wva, wvb = _blockdiag_first(cmp_v_w1[l])
        cw = jnp.pad(conv_w[l], ((0, 8 - CONV_K), (0, 0)))

        main, kc, vc, kvsw, gl = _inproj(x2, norms[l, 0][None, :], w_all, tm)
        kcmp, vcmp = _compress(kc.reshape(b, nch, CMP_STRIDE * KV_WIDTH),
                               vc.reshape(b, nch, CMP_STRIDE * KV_WIDTH), pe2,
                               wka, wkb, _blockdiag_second(cmp_k_w2[l]),
                               wva, wvb, _blockdiag_second(cmp_v_w2[l]), ncp)
        o_nsa = _nsa(far_hl, main, gl, kcmp, vcmp, kvsw, bias_c, near, ovt, etab, eye, b, s)
        x2 = _mixout(x2, o_nsa, main, cw, gnn, gn_conv[l][None, :], wo, norms[l, 1][None, :], s, tm)

        memkv = _norm_matmul(mem2, mem_norm[l][None, :], mem_wkv[l].astype(BF16), ml, BF16, "mem_kv")
        x2 = _memattn(x2, norms[l, 2][None, :], (mem_wq[l] * (MEM_HEAD_DIM ** -0.5)).astype(BF16),
                      memkv.reshape(b, ml, 2 * MEM_WIDTH), mem_wo[l].astype(BF16), norms[l, 3][None, :], s, tm)

        x2 = _ffn(x2, norms[l, 4][None, :], ffn_wg[l].astype(BF16), ffn_wu[l].astype(BF16),
                  ffn_wd[l].astype(BF16), norms[l, 5][None, :], tm, 256)
    return x2.reshape(b, s, d)
```

```python
import functools
import math

import numpy as np
import jax
import jax.numpy as jnp
from jax import lax
from jax.experimental import pallas as pl
from jax.experimental.pallas import tpu as pltpu

F32 = jnp.float32
BF16 = jnp.bfloat16

HEAD_DIM = 64
NSA_HEADS = 8
KV_GROUPS = 2
HPG = NSA_HEADS // KV_GROUPS
NSA_WIDTH = NSA_HEADS * HEAD_DIM
KV_WIDTH = KV_GROUPS * HEAD_DIM
CONV_WIDTH = 512
CONV_K = 3
CMP_STRIDE = 16
CMP_BLOCK = 32
CMP_HIDDEN = 128
SLC_BLOCK = 64
SLC_TOPK = 8
WINDOW = 256
FORCE_BONUS = 1.0e4
N_BUCKETS = 32
MAX_DISTANCE = 128
MEM_HEADS = 4
MEM_HEAD_DIM = 64
MEM_WIDTH = MEM_HEADS * MEM_HEAD_DIM
RMS_EPS = 1e-6
NEG_INF = -1e30
GATE_LANES = 128

TQ = 256
RC = 64
NT_DIMS = (((1,), (1,)), ((), ()))
LOG2E = math.log2(math.e)

VMEM_LIMIT = 56 * 1024 * 1024


def _rms(x, g):
    return x * lax.rsqrt(jnp.mean(x * x, axis=-1, keepdims=True) + RMS_EPS) * g


def _dot(a, b):
    return jnp.dot(a, b, preferred_element_type=F32)


def _dot_nt(a, b):
    return lax.dot_general(a, b, NT_DIMS, preferred_element_type=F32)


def _sigmoid(x):
    return 1.0 / (1.0 + jnp.exp(-x))


def _params(*sem):
    return pltpu.CompilerParams(dimension_semantics=sem, vmem_limit_bytes=VMEM_LIMIT)


def _const_spec(shape, single=False):
    index_map = lambda *_: (0,) * len(shape)
    if single:
        return pl.BlockSpec(shape, index_map, pipeline_mode=pl.Buffered(1))
    return pl.BlockSpec(shape, index_map)


def _inproj_kernel(x_ref, g_ref, w_ref, main_ref, kc_ref, vc_ref, kvsw_ref, gl_ref):
    xn = _rms(x_ref[...], g_ref[...]).astype(BF16)

    def mm(lo, hi):
        return _dot(xn, w_ref[:, lo:hi])

    for c in range(4):
        main_ref[:, 512 * c:512 * (c + 1)] = mm(512 * c, 512 * (c + 1)).astype(BF16)
    r = mm(2048, 2304)
    kc_ref[...] = r[:, :KV_WIDTH].astype(BF16)
    vc_ref[...] = r[:, KV_WIDTH:].astype(BF16)
    kvsw_ref[...] = mm(2304, 2816).astype(BF16)
    gl_ref[...] = mm(2816, 2944)


def _inproj(x2, g, w, tm):
    t, d = x2.shape
    nw = w.shape[1]
    return pl.pallas_call(
        _inproj_kernel,
        grid=(t // tm,),
        in_specs=[pl.BlockSpec((tm, d), lambda i: (i, 0)),
                  _const_spec((1, d)),
                  _const_spec((d, nw))],
        out_specs=[pl.BlockSpec((tm, 2048), lambda i: (i, 0)),
                   pl.BlockSpec((tm, KV_WIDTH), lambda i: (i, 0)),
                   pl.BlockSpec((tm, KV_WIDTH), lambda i: (i, 0)),
                   pl.BlockSpec((tm, 4 * KV_WIDTH), lambda i: (i, 0)),
                   pl.BlockSpec((tm, GATE_LANES), lambda i: (i, 0))],
        out_shape=[jax.ShapeDtypeStruct((t, 2048), BF16),
                   jax.ShapeDtypeStruct((t, KV_WIDTH), BF16),
                   jax.ShapeDtypeStruct((t, KV_WIDTH), BF16),
                   jax.ShapeDtypeStruct((t, 4 * KV_WIDTH), BF16),
                   jax.ShapeDtypeStruct((t, GATE_LANES), F32)],
        compiler_params=_params("parallel"),
        name="inproj",
    )(x2, g, w)


def _norm_matmul_kernel(x_ref, g_ref, w_ref, o_ref):
    xn = _rms(x_ref[...], g_ref[...]).astype(BF16)
    o_ref[...] = _dot(xn, w_ref[...]).astype(o_ref.dtype)


def _norm_matmul(x2, g, w, tm, out_dtype, name):
    t, d = x2.shape
    n = w.shape[1]
    return pl.pallas_call(
        _norm_matmul_kernel,
        grid=(t // tm,),
        in_specs=[pl.BlockSpec((tm, d), lambda i: (i, 0)),
                  _const_spec((1, d)),
                  _const_spec((d, n))],
        out_specs=pl.BlockSpec((tm, n), lambda i: (i, 0)),
        out_shape=jax.ShapeDtypeStruct((t, n), out_dtype),
        compiler_params=_params("parallel"),
        name=name,
    )(x2, g, w)


def _gelu_tanh(x):
    c = math.sqrt(2.0 / math.pi)
    return x * (0.5 * (1.0 + jnp.tanh(c * (x + 0.044715 * (x * x * x)))))


def _compress_kernel(kc_ref, vc_ref, pe_ref, wka_ref, wkb_ref, wk2_ref, wva_ref, wvb_ref, wv2_ref,
                     ko_ref, vo_ref, *, nch, ncp):
    row = lax.broadcasted_iota(jnp.int32, (nch, KV_WIDTH), 0)

    def one(x_ref, wa_ref, wb_ref, w2_ref, o_ref):
        x = x_ref[0].astype(F32)
        a = _dot((x + pe_ref[0:1, :]).astype(BF16), wa_ref[...])
        b = _dot((x + pe_ref[1:2, :]).astype(BF16), wb_ref[...])
        hid = _gelu_tanh(a + pltpu.roll(b, nch - 1, 0))
        out = _dot(hid.astype(BF16), w2_ref[...])
        out = jnp.where(row < nch - 1, out, 0.0).astype(BF16)
        if ncp > nch:
            out = jnp.concatenate([out, jnp.zeros((ncp - nch, KV_WIDTH), BF16)], axis=0)
        o_ref[0] = out

    one(kc_ref, wka_ref, wkb_ref, wk2_ref, ko_ref)
    one(vc_ref, wva_ref, wvb_ref, wv2_ref, vo_ref)


def _compress(kc3, vc3, pe2, wka, wkb, wk2, wva, wvb, wv2, ncp):
    b, nch, cw = kc3.shape
    hid2 = wka.shape[1]
    xspec = pl.BlockSpec((1, nch, cw), lambda i: (i, 0, 0))
    ospec = pl.BlockSpec((1, ncp, KV_WIDTH), lambda i: (i, 0, 0))
    return pl.pallas_call(
        functools.partial(_compress_kernel, nch=nch, ncp=ncp),
        grid=(b,),
        in_specs=[xspec, xspec, _const_spec((2, cw)),
                  _const_spec((cw, hid2)), _const_spec((cw, hid2)), _const_spec((hid2, KV_WIDTH)),
                  _const_spec((cw, hid2)), _const_spec((cw, hid2)), _const_spec((hid2, KV_WIDTH))],
        out_specs=[ospec, ospec],
        out_shape=[jax.ShapeDtypeStruct((b, ncp, KV_WIDTH), BF16)] * 2,
        compiler_params=_params("parallel"),
        name="compress",
    )(kc3, vc3, pe2, wka, wkb, wk2, wva, wvb, wv2)


def _nsa_kernel(far_ref, q_ref, gl_ref, kc_ref, vc_ref, ks_ref, vs_ref, kw_ref, vw_ref,
                bc_ref, nt_ref, ovt_ref, et_ref, eye_ref, o_ref,
                qa_scr, s_all, p_scr, mx_scr, m_scr, acc_scr,
                sw_scr, pw_scr, mxw_scr, mw_scr, accw_scr, out_scr, g_scr, sc_scr, *, ncp, nbp, seq):
    i = pl.program_id(1)
    t0 = i * TQ
    grows = HPG * TQ
    rows = KV_GROUPS * grows
    jprev = jnp.maximum(i - 1, 0)
    n_far = jnp.maximum(i - 1, 0)
    n_chunks = rows // RC

    lane = lax.broadcasted_iota(jnp.int32, (TQ, KV_WIDTH), 1)
    lo_half = lane < HEAD_DIM

    blk = lax.broadcasted_iota(jnp.int32, (nbp, TQ), 0)
    cur = (t0 + lax.broadcasted_iota(jnp.int32, (nbp, TQ), 1)) >> 6
    valid = blk <= cur
    forced = (blk == 0) | (blk == cur) | (blk == cur - 1)
    bonus = jnp.where(forced, FORCE_BONUS, 0.0).astype(F32)

    gates = _sigmoid(gl_ref[...])

    def ktile(ref, j):
        return ref[0, pl.ds(pl.multiple_of(j * TQ, TQ), TQ), :]

    def key_aug(ref, j, block_mask):
        extra = et_ref[pl.ds(pl.multiple_of(j * TQ, TQ), TQ), :] if block_mask else et_ref[seq:seq + TQ, :]
        return jnp.concatenate([ktile(ref, j), extra], axis=1)

    prev_gate = jnp.where(i > 0, 0.0, NEG_INF).astype(F32)

    def logits(br, slot, k_aug, table_off, first, gated=False):
        s_ref, _, mx_ref, _, _ = br
        sc = _dot_nt(qa_scr[...], k_aug)
        if table_off is not None:
            sc = sc + nt_ref[:, table_off:table_off + TQ]
        if gated:
            sc = sc + prev_gate
        s_ref[slot] = sc
        part = jnp.maximum(sc[:, 0:KV_WIDTH], sc[:, KV_WIDTH:])
        mx_ref[...] = part if first else jnp.maximum(mx_ref[...], part)

    def set_max(br):
        _, _, mx_ref, m_ref, _ = br
        m_ref[...] = jnp.broadcast_to(jnp.max(mx_ref[...], axis=1, keepdims=True), (rows, KV_WIDTH))

    ones_cols = jnp.ones((TQ, KV_WIDTH), BF16)

    def accumulate(br, slot, v_tile, first):
        s_ref, p_ref, _, m_ref, acc_ref = br
        for c in range(n_chunks):
            sl = slice(c * RC, (c + 1) * RC)
            m = m_ref[sl, :]
            e = jnp.exp2(s_ref[slot, sl, :] - jnp.concatenate([m, m], axis=1))
            p_ref[sl, :] = e.astype(BF16)
        pv = _dot(p_ref[...], jnp.concatenate([v_tile, ones_cols], axis=1))
        acc_ref[...] = pv if first else acc_ref[...] + pv

    def gate_rows(k):
        return jnp.concatenate(
            [jnp.broadcast_to(gates[:, 3 * h + k:3 * h + k + 1], (TQ, KV_WIDTH)) for h in range(NSA_HEADS)],
            axis=0)

    def branch_out(br, k):
        acc_ref = br[4]
        return g_scr[k] * (acc_ref[:, 0:KV_WIDTH] / acc_ref[:, KV_WIDTH:])

    sel_br = (s_all, p_scr, mx_scr, m_scr, acc_scr)
    win_br = (sw_scr, pw_scr, mxw_scr, mw_scr, accw_scr)

    for k in range(3):
        g_scr[k] = gate_rows(k)

    for h in range(NSA_HEADS):
        g, p = divmod(h, HPG)
        keep = lo_half if g == 0 else jnp.logical_not(lo_half)
        qa_scr[h * TQ:(h + 1) * TQ, 0:KV_WIDTH] = jnp.where(
            keep, q_ref[:, KV_WIDTH * p:KV_WIDTH * (p + 1)], jnp.zeros((), BF16))
        far_lanes = jnp.where(lane == nbp, far_ref[0, h], jnp.where(lane == nbp + 1, far_ref[1, h], 0.0))
        qa_scr[h * TQ:(h + 1) * TQ, KV_WIDTH:2 * KV_WIDTH] = far_lanes.astype(BF16)

    logits(win_br, 0, key_aug(kw_ref, i, False), 0, True)
    logits(win_br, 1, key_aug(kw_ref, jprev, False), 2 * TQ, False, gated=True)
    set_max(win_br)
    accumulate(win_br, 0, ktile(vw_ref, i), True)
    accumulate(win_br, 1, ktile(vw_ref, jprev), False)

    s_all[0, :, 0:ncp] = _dot_nt(qa_scr[:, 0:KV_WIDTH], kc_ref[0])
    for c in range(n_chunks):
        sl = slice(c * RC, (c + 1) * RC)
        h, r0 = divmod(c * RC, TQ)
        s = s_all[0, sl, 0:ncp] + bc_ref[h, r0:r0 + RC, :]
        e = jnp.exp2(s - jnp.broadcast_to(jnp.max(s, axis=1, keepdims=True), (RC, ncp)))
        s_all[0, sl, 0:ncp] = e
        p_scr[sl, 0:ncp] = e.astype(BF16)
    pv = _dot(p_scr[:, 0:ncp], jnp.concatenate([vc_ref[0], jnp.ones((ncp, KV_WIDTH), BF16)], axis=1))
    tq = t0 + (lax.broadcasted_iota(jnp.int32, (rows, KV_WIDTH), 0) & (TQ - 1))
    inv = jnp.where(tq >= CMP_BLOCK - 1, 1.0 / jnp.maximum(pv[:, KV_WIDTH:], 1e-30), 0.0)
    out_scr[...] = g_scr[0] * (pv[:, 0:KV_WIDTH] * inv)
    s_all[0, :, 0:ncp] = s_all[0, :, 0:ncp] * inv

    for g in range(KV_GROUPS):
        psum = s_all[0, g * grows:g * grows + TQ, 0:ncp]
        for p in range(1, HPG):
            psum = psum + s_all[0, g * grows + p * TQ:g * grows + (p + 1) * TQ, 0:ncp]
        hi = psum.astype(BF16)
        r1 = psum - hi.astype(F32)
        mid = r1.astype(BF16)
        lo = (r1 - mid.astype(F32)).astype(BF16)
        ovt = ovt_ref[...]
        imp_t = _dot_nt(ovt, hi) + _dot_nt(ovt, mid) + _dot_nt(ovt, lo)
        sc_scr[g * nbp:(g + 1) * nbp, :] = jnp.where(valid, imp_t + bonus, NEG_INF)

    def rank_body(it, cnts):
        new = []
        for g in range(KV_GROUPS):
            score = sc_scr[g * nbp:(g + 1) * nbp, :]
            cnt = cnts[g]
            for k in range(TQ // SLC_BLOCK):
                ii = it * (TQ // SLC_BLOCK) + k
                si = sc_scr[pl.ds(g * nbp + ii, 1), :]
                beats = (si > score) | ((si == score) & (blk > ii))
                cnt = cnt + beats.astype(jnp.int32)
            new.append(cnt)
        return tuple(new)

    cnts = lax.fori_loop(0, i + 1, rank_body, (jnp.zeros((nbp, TQ), jnp.int32),) * KV_GROUPS)

    for g in range(KV_GROUPS):
        sel = valid & (cnts[g] < SLC_TOPK)
        mneg_t = jnp.where(sel, 0.0, NEG_INF).astype(BF16)
        mneg_t = jnp.concatenate([mneg_t, jnp.zeros((KV_WIDTH - nbp, TQ), BF16)], axis=0)
        mneg = _dot_nt(eye_ref[...], mneg_t)
        for p in range(HPG):
            h = g * HPG + p
            extra = jnp.where(lane == nbp, far_ref[0, h], jnp.where(lane == nbp + 1, far_ref[1, h], mneg))
            qa_scr[h * TQ:(h + 1) * TQ, KV_WIDTH:2 * KV_WIDTH] = extra.astype(BF16)

    prev_slot = jnp.where(i > 0, i - 1, 1)
    logits(sel_br, i, key_aug(ks_ref, i, True), 0, True)
    logits(sel_br, prev_slot, key_aug(ks_ref, jprev, True), TQ, False, gated=True)

    def far_pairs(fn):
        def pair(k, carry):
            fn(2 * k)
            fn(2 * k + 1)
            return carry

        lax.fori_loop(0, n_far // 2, pair, 0)

        @pl.when(n_far % 2 == 1)
        def _():
            fn(n_far - 1)

    far_pairs(lambda j: logits(sel_br, j, key_aug(ks_ref, j, True), None, False))
    set_max(sel_br)
    accumulate(sel_br, i, ktile(vs_ref, i), True)
    accumulate(sel_br, prev_slot, ktile(vs_ref, jprev), False)
    far_pairs(lambda j: accumulate(sel_br, j, ktile(vs_ref, j), False))

    out = out_scr[...] + branch_out(sel_br, 1) + branch_out(win_br, 2)
    for p in range(HPG):
        o_ref[:, KV_WIDTH * p:KV_WIDTH * (p + 1)] = jnp.where(
            lo_half, out[p * TQ:(p + 1) * TQ, :], out[grows + p * TQ:grows + (p + 1) * TQ, :]
        ).astype(BF16)


def _nsa(far_hl, main, gl, kcmp, vcmp, kvsw, bias_c, near, ovt, etab, eye, b, s):
    ncp = kcmp.shape[1]
    nbp = ovt.shape[0]
    nq = s // TQ
    rows = NSA_HEADS * TQ
    kv_spec = lambda col: pl.BlockSpec((1, s, KV_WIDTH), lambda bi, i: (bi, 0, col))
    kvsw3 = kvsw.reshape(b, s, 4 * KV_WIDTH)
    return pl.pallas_call(
        functools.partial(_nsa_kernel, ncp=ncp, nbp=nbp, seq=s),
        grid=(b, nq),
        in_specs=[pl.BlockSpec(memory_space=pltpu.SMEM),
                  pl.BlockSpec((TQ, NSA_WIDTH), lambda bi, i: (bi * nq + i, 0)),
                  pl.BlockSpec((TQ, GATE_LANES), lambda bi, i: (bi * nq + i, 0)),
                  pl.BlockSpec((1, ncp, KV_WIDTH), lambda bi, i: (bi, 0, 0)),
                  pl.BlockSpec((1, ncp, KV_WIDTH), lambda bi, i: (bi, 0, 0)),
                  kv_spec(0), kv_spec(1), kv_spec(2), kv_spec(3),
                  pl.BlockSpec((NSA_HEADS, TQ, ncp), lambda bi, i: (0, i, 0)),
                  _const_spec((rows, 3 * TQ), single=True),
                  _const_spec((nbp, ncp), single=True),
                  _const_spec((s + TQ, KV_WIDTH), single=True),
                  _const_spec((TQ, TQ), single=True)],
        out_specs=pl.BlockSpec((TQ, NSA_WIDTH), lambda bi, i: (bi * nq + i, 0)),
        out_shape=jax.ShapeDtypeStruct((b * s, NSA_WIDTH), BF16),
        scratch_shapes=[pltpu.VMEM((rows, 2 * KV_WIDTH), BF16),
                        pltpu.VMEM((nq, rows, TQ), F32),
                        pltpu.VMEM((rows, TQ), BF16),
                        pltpu.VMEM((rows, KV_WIDTH), F32),
                        pltpu.VMEM((rows, KV_WIDTH), F32),
                        pltpu.VMEM((rows, 2 * KV_WIDTH), F32),
                        pltpu.VMEM((2, rows, TQ), F32),
                        pltpu.VMEM((rows, TQ), BF16),
                        pltpu.VMEM((rows, KV_WIDTH), F32),
                        pltpu.VMEM((rows, KV_WIDTH), F32),
                        pltpu.VMEM((rows, 2 * KV_WIDTH), F32),
                        pltpu.VMEM((rows, KV_WIDTH), F32),
                        pltpu.VMEM((3, rows, KV_WIDTH), F32),
                        pltpu.VMEM((KV_GROUPS * nbp, TQ), F32)],
        compiler_params=_params("parallel", "arbitrary"),
        name="nsa_attention",
    )(far_hl, main, gl, kcmp, vcmp, kvsw3, kvsw3, kvsw3, kvsw3, bias_c, near, ovt, etab, eye)


def _mixout_kernel(x_ref, on_ref, cb_ref, cc_ref, cx_ref, hc_ref, hx_ref, cw_ref, gnn_ref, gnc_ref,
                   wo_ref, g1_ref, out_ref, *, tiles_per_seq, tm):
    first = (pl.program_id(0) % tiles_per_seq) == 0
    u = cc_ref[...].astype(F32) * cx_ref[...].astype(F32)
    halo = hc_ref[...].astype(F32) * hx_ref[...].astype(F32)
    halo = jnp.where(first, 0.0, halo)
    ue = jnp.concatenate([halo, u], axis=0)
    u1 = pltpu.roll(ue, 1, 0)[8:]
    u2 = pltpu.roll(ue, 2, 0)[8:]
    y = cw_ref[0:1, :] * u + cw_ref[1:2, :] * u1 + cw_ref[2:3, :] * u2
    o_conv = cb_ref[...].astype(F32) * y
    merged = jnp.concatenate([_rms(on_ref[...].astype(F32), gnn_ref[...]),
                              _rms(o_conv, gnc_ref[...])], axis=-1).astype(BF16)
    h = _dot(merged, wo_ref[...])
    out_ref[...] = x_ref[...] + _rms(h, g1_ref[...])


def _mixout(x2, o_nsa, main, cw, gnn, gnc, wo, g1, s, tm):
    t, d = x2.shape
    tps = s // tm
    hb = tm // 8
    halo_spec = lambda col: pl.BlockSpec((8, CONV_WIDTH), lambda i: (jnp.maximum(i * hb - 1, 0), col))
    return pl.pallas_call(
        functools.partial(_mixout_kernel, tiles_per_seq=tps, tm=tm),
        grid=(t // tm,),
        in_specs=[pl.BlockSpec((tm, d), lambda i: (i, 0)),
                  pl.BlockSpec((tm, NSA_WIDTH), lambda i: (i, 0)),
                  pl.BlockSpec((tm, CONV_WIDTH), lambda i: (i, 1)),
                  pl.BlockSpec((tm, CONV_WIDTH), lambda i: (i, 2)),
                  pl.BlockSpec((tm, CONV_WIDTH), lambda i: (i, 3)),
                  halo_spec(2), halo_spec(3),
                  _const_spec((8, CONV_WIDTH)),
                  _const_spec((1, NSA_WIDTH)), _const_spec((1, CONV_WIDTH)),
                  _const_spec((NSA_WIDTH + CONV_WIDTH, d)),
                  _const_spec((1, d))],
        out_specs=pl.BlockSpec((tm, d), lambda i: (i, 0)),
        out_shape=jax.ShapeDtypeStruct((t, d), F32),
        compiler_params=_params("parallel"),
        name="mix_out",
    )(x2, o_nsa, main, main, main, main, main, cw, gnn, gnc, wo, g1)


def _memattn_kernel(x_ref, g2_ref, wq_ref, kv_ref, wo_ref, g3_ref, out_ref, *, tm):
    x = x_ref[...]
    q = _dot(_rms(x, g2_ref[...]).astype(BF16), wq_ref[...]).astype(BF16)
    lane = lax.broadcasted_iota(jnp.int32, (tm, 2 * MEM_HEAD_DIM), 1)
    lo_half = lane < MEM_HEAD_DIM
    zero = jnp.zeros((), BF16)
    slots = []
    for sl in range(MEM_HEADS // 2):
        qs = q[:, 128 * sl:128 * (sl + 1)]
        q2 = jnp.concatenate([jnp.where(lo_half, qs, zero), jnp.where(lo_half, zero, qs)], axis=0)
        k = kv_ref[0, :, 128 * sl:128 * (sl + 1)]
        v = kv_ref[0, :, MEM_WIDTH + 128 * sl:MEM_WIDTH + 128 * (sl + 1)]
        lg = _dot_nt(q2, k)
        e = jnp.exp(lg - jnp.max(lg, axis=-1, keepdims=True))
        p = e / jnp.sum(e, axis=-1, keepdims=True)
        o2 = _dot(p.astype(BF16), v)
        slots.append(jnp.where(lo_half, o2[:tm], o2[tm:]))
    o = jnp.concatenate(slots, axis=-1).astype(BF16)
    h = _dot(o, wo_ref[...])
    out_ref[...] = x + _rms(h, g3_ref[...])


def _memattn(x2, g2, wq, memkv, wo, g3, s, tm):
    t, d = x2.shape
    tps = s // tm
    ml = memkv.shape[1]
    return pl.pallas_call(
        functools.partial(_memattn_kernel, tm=tm),
        grid=(t // tm,),
        in_specs=[pl.BlockSpec((tm, d), lambda i: (i, 0)),
                  _const_spec((1, d)),
                  _const_spec((d, MEM_WIDTH)),
                  pl.BlockSpec((1, ml, 2 * MEM_WIDTH), lambda i: (i // tps, 0, 0)),
                  _const_spec((MEM_WIDTH, d)),
                  _const_spec((1, d))],
        out_specs=pl.BlockSpec((tm, d), lambda i: (i, 0)),
        out_shape=jax.ShapeDtypeStruct((t, d), F32),
        compiler_params=_params("parallel"),
        name="mem_attention",
    )(x2, g2, wq, memkv, wo, g3)


def _ffn_kernel(x_ref, g4_ref, wg_ref, wu_ref, wd_ref, g5_ref, out_ref, hid_ref, *, th):
    xn = _rms(x_ref[...], g4_ref[...]).astype(BF16)
    for c in range(hid_ref.shape[1] // th):
        cols = slice(c * th, (c + 1) * th)
        a = _dot(xn, wg_ref[:, cols])
        u = _dot(xn, wu_ref[:, cols])
        hid_ref[:, cols] = (a * _sigmoid(a) * u).astype(BF16)
    h = _dot(hid_ref[...], wd_ref[...])
    out_ref[...] = x_ref[...] + _rms(h, g5_ref[...])


def _ffn(x2, g4, wg, wu, wd, g5, tm, th):
    t, d = x2.shape
    hdim = wg.shape[1]
    return pl.pallas_call(
        functools.partial(_ffn_kernel, th=th),
        grid=(t // tm,),
        in_specs=[pl.BlockSpec((tm, d), lambda i: (i, 0)),
                  _const_spec((1, d)),
                  _const_spec((d, hdim), single=True),
                  _const_spec((d, hdim), single=True),
                  _const_spec((hdim, d), single=True),
                  _const_spec((1, d))],
        out_specs=pl.BlockSpec((tm, d), lambda i: (i, 0)),
        out_shape=jax.ShapeDtypeStruct((t, d), F32),
        scratch_shapes=[pltpu.VMEM((tm, hdim), BF16)],
        compiler_params=_params("parallel"),
        name="swiglu",
    )(x2, g4, wg, wu, wd, g5)


def _t5_bucket(dist):
    n = jnp.maximum(dist, 0)
    max_exact = N_BUCKETS // 2
    nf = jnp.maximum(n, 1).astype(F32)
    large = max_exact + (jnp.log(nf / max_exact) / math.log(MAX_DISTANCE / max_exact)
                         * (N_BUCKETS - max_exact)).astype(jnp.int32)
    large = jnp.minimum(large, N_BUCKETS - 1)
    return jnp.where(n < max_exact, n, large)


def _bias_table(rel_bias, dist, visible, minus=None):
    bucket = _t5_bucket(dist)[None]
    out = jnp.zeros((rel_bias.shape[0],) + dist.shape, F32)
    for k in range(N_BUCKETS):
        out = jnp.where(bucket == k, rel_bias[:, k].reshape((-1,) + (1,) * dist.ndim), out)
    if minus is not None:
        out = out - minus.reshape((-1,) + (1,) * dist.ndim)
    return jnp.where(visible[None], out * LOG2E, NEG_INF)


def _pair_heads(a, axis):
    chunks = jnp.split(a, NSA_HEADS, axis=axis)
    order = []
    for p in range(HPG):
        order += [chunks[p], chunks[HPG + p]]
    return jnp.concatenate(order, axis=axis)


def _blockdiag_first(w1):
    hdim = w1.shape[1]
    w1r = w1.reshape(CMP_BLOCK, HEAD_DIM, hdim)
    eye = jnp.eye(KV_GROUPS, dtype=w1.dtype)
    halves = []
    for part in (w1r[:CMP_STRIDE], w1r[CMP_STRIDE:]):
        halves.append(jnp.einsum('jdm,gh->jgdhm', part, eye)
                      .reshape(CMP_STRIDE * KV_WIDTH, KV_GROUPS * hdim).astype(BF16))
    return halves


def _blockdiag_second(w2):
    eye = jnp.eye(KV_GROUPS, dtype=w2.dtype)
    return jnp.einsum('md,gh->gmhd', w2, eye).reshape(KV_GROUPS * w2.shape[0], KV_WIDTH).astype(BF16)


def kernel(x, mem, rel_bias, norms, mem_norm, gn_nsa, gn_conv, w_in, w_out, cmp_pe, cmp_k_w1, cmp_k_w2,
           cmp_v_w1, cmp_v_w2, conv_w, mem_wq, mem_wkv, mem_wo, ffn_wg, ffn_wu, ffn_wd):
    b, s, d = x.shape
    depth = w_in.shape[0]
    ml = mem.shape[1]
    t = b * s
    assert s % TQ == 0 and s // SLC_BLOCK <= 32 and TQ == WINDOW
    nch = s // CMP_STRIDE
    ncp = -(-nch // 128) * 128
    assert ncp == KV_WIDTH and s >= 2 * TQ
    n_cmp = nch - 1
    n_slc = s // SLC_BLOCK
    nbp = 32
    tm = 512

    r = jnp.arange(TQ)[:, None]
    c = jnp.arange(TQ)[None, :]
    far = rel_bias[:, N_BUCKETS - 1]
    near = jnp.concatenate([_bias_table(rel_bias, r - c, r >= c, far),
                            _bias_table(rel_bias, r + TQ - c, r + TQ - c > 0, far),
                            _bias_table(rel_bias, r + TQ - c, r < c, far)],
                           axis=-1).reshape(NSA_HEADS * TQ, 3 * TQ)
    dist_c = jnp.arange(s)[:, None] - CMP_STRIDE * jnp.arange(ncp)[None, :] - (CMP_BLOCK - 1)
    bias_c = _bias_table(rel_bias, dist_c, dist_c >= 0)
    far_hi = (far * LOG2E).astype(BF16).astype(F32)
    far_lo = (far * LOG2E - far_hi).astype(BF16).astype(F32)
    far_hl = jnp.stack([far_hi, far_lo])
    cs = np.arange(ncp) * CMP_STRIDE
    ss = np.arange(nbp) * SLC_BLOCK
    ov = np.clip(np.minimum(cs[:, None] + CMP_BLOCK, ss[None, :] + SLC_BLOCK)
                 - np.maximum(cs[:, None], ss[None, :]), 0, None) / CMP_BLOCK
    ov[n_cmp:, :] = 0.0
    ov[:, n_slc:] = 0.0
    ovt = jnp.asarray(ov.T, dtype=BF16)
    et = np.zeros((s + TQ, KV_WIDTH), np.float32)
    et[np.arange(s), np.arange(s) // SLC_BLOCK] = 1.0
    et[:, nbp:nbp + 2] = 1.0
    etab = jnp.asarray(et, dtype=BF16)
    eye = jnp.eye(TQ, dtype=BF16)

    x2 = x.reshape(t, d)
    mem2 = mem.reshape(b * ml, d)
    for l in range(depth):
        wq, wkc, wvc, wks, wvs, wkw, wvw, wgl, wcb, wcc, wcx = jnp.split(
            w_in[l], np.cumsum((NSA_WIDTH,) + (KV_WIDTH,) * 6 + (3 * NSA_HEADS,) + (CONV_WIDTH,) * 2), axis=1)
        wgl = jnp.pad(wgl, ((0, 0), (0, GATE_LANES - wgl.shape[1])))
        w_all = jnp.concatenate([_pair_heads(wq, 1) * (HEAD_DIM ** -0.5 * LOG2E), wcb, wcc, wcx,
                                 wkc, wvc, wks, wvs, wkw, wvw, wgl], axis=1).astype(BF16)
        gnn = _pair_heads(gn_nsa[l], 0)[None, :]
        wo = jnp.concatenate([_pair_heads(w_out[l][:NSA_WIDTH], 0), w_out[l][NSA_WIDTH:]], axis=0).astype(BF16)
        pe = cmp_pe[l].reshape(2, CMP_STRIDE, 1, HEAD_DIM)
        pe2 = jnp.broadcast_to(pe, (2, CMP_STRIDE, KV_GROUPS, HEAD_DIM)).reshape(2, CMP_STRIDE * KV_WIDTH)
        wka, wkb = _blockdiag_first(cmp_k_w1[l])
        wva, wvb = _blockdiag_first(cmp_v_w1[l])
        cw = jnp.pad(conv_w[l], ((0, 8 - CONV_K), (0, 0)))

        main, kc, vc, kvsw, gl = _inproj(x2, norms[l, 0][None, :], w_all, tm)
        kcmp, vcmp = _compress(kc.reshape(b, nch, CMP_STRIDE * KV_WIDTH),
                               vc.reshape(b, nch, CMP_STRIDE * KV_WIDTH), pe2,
                               wka, wkb, _blockdiag_second(cmp_k_w2[l]),
                               wva, wvb, _blockdiag_second(cmp_v_w2[l]), ncp)
        o_nsa = _nsa(far_hl, main, gl, kcmp, vcmp, kvsw, bias_c, near, ovt, etab, eye, b, s)
        x2 = _mixout(x2, o_nsa, main, cw, gnn, gn_conv[l][None, :], wo, norms[l, 1][None, :], s, tm)

        memkv = _norm_matmul(mem2, mem_norm[l][None, :], mem_wkv[l].astype(BF16), ml, BF16, "mem_kv")
        x2 = _memattn(x2, norms[l, 2][None, :], (mem_wq[l] * (MEM_HEAD_DIM ** -0.5)).astype(BF16),
                      memkv.reshape(b, ml, 2 * MEM_WIDTH), mem_wo[l].astype(BF16), norms[l, 3][None, :], s, tm)

        x2 = _ffn(x2, norms[l, 4][None, :], ffn_wg[l].astype(BF16), ffn_wu[l].astype(BF16),
                  ffn_wd[l].astype(BF16), norms[l, 5][None, :], tm, 256)
    return x2.reshape(b, s, d)
```
